```python
import jax, jax.numpy as jnp
from jax import lax
import numpy as np

D_MODEL = 2048
BATCH = 4
SEQ = 4096
DEPTH = 1

CHUNK = 64
Q_BLOCK = 128
EPS = 1e-6

MLA_HEADS = 8
MLA_Q_RANK = 512
MLA_KV_RANK = 256
MLA_NOPE = 128
MLA_ROPE = 64
MLA_V = 128
ROPE_THETA = 10000.0

SB_HEADS = 8
SB_DIM = 128
SB_WIDTH = SB_HEADS * SB_DIM

PEER_HEADS = 8
PEER_NKEYS = 128
PEER_N = PEER_NKEYS * PEER_NKEYS
PEER_DKEY = 256
PEER_TOPK = 16
PEER_TOK_BLOCK = 128

IN_SIZES = (MLA_Q_RANK, MLA_KV_RANK, MLA_ROPE, SB_WIDTH, SB_WIDTH, SB_WIDTH, D_MODEL, D_MODEL)
IN_COLS = MLA_Q_RANK + MLA_KV_RANK + MLA_ROPE + 3 * SB_WIDTH + 2 * D_MODEL
N_ADA = 6

kernel_name = "hybrid_mla_stickbreak_peer_block"


def rmsnorm(x, g):
    xf = x.astype(jnp.float32)
    y = xf * lax.rsqrt(jnp.mean(xf * xf, axis=-1, keepdims=True) + EPS)
    return (y * g.astype(jnp.float32)).astype(x.dtype)


def rope(x, cos, sin):
    half = x.shape[-1] // 2
    x1, x2 = x[..., :half], x[..., half:]
    return jnp.concatenate([x1 * cos - x2 * sin, x2 * cos + x1 * sin], axis=-1)


def mla_attention(q_nope, q_rope, k_nope, k_rope, v):
    B, S, H, _ = q_nope.shape
    nb = S // Q_BLOCK
    qn = q_nope.reshape(B, nb, Q_BLOCK, H, MLA_NOPE).transpose(1, 0, 2, 3, 4)
    qr = q_rope.reshape(B, nb, Q_BLOCK, H, MLA_ROPE).transpose(1, 0, 2, 3, 4)
    q0s = jnp.arange(nb, dtype=jnp.int32) * Q_BLOCK
    key_chunk = jnp.arange(S, dtype=jnp.int32) // CHUNK
    scale = (MLA_NOPE + MLA_ROPE) ** -0.5

    def block(args):
        qn_b, qr_b, q0 = args
        s = (jnp.einsum('bqhd,bkhd->bhqk', qn_b, k_nope)
             + jnp.einsum('bqhd,bkd->bhqk', qr_b, k_rope)).astype(jnp.float32) * scale
        q_chunk = (q0 + jnp.arange(Q_BLOCK, dtype=jnp.int32)) // CHUNK
        allowed = key_chunk[None, :] <= q_chunk[:, None]
        s = jnp.where(allowed, s, -jnp.inf)
        p = jax.nn.softmax(s, axis=-1).astype(v.dtype)
        return jnp.einsum('bhqk,bkhd->bqhd', p, v)

    out = lax.map(block, (qn, qr, q0s))
    return out.transpose(1, 0, 2, 3, 4).reshape(B, S, H * MLA_V)


def stick_breaking_attention(q, k, v):
    B, S, H, Dh = q.shape
    nb = S // Q_BLOCK
    qb = q.reshape(B, nb, Q_BLOCK, H, Dh).transpose(1, 0, 2, 3, 4)
    q0s = jnp.arange(nb, dtype=jnp.int32) * Q_BLOCK
    key_pos = jnp.arange(S, dtype=jnp.int32)
    scale = Dh ** -0.5

    def block(args):
        q_b, q0 = args
        z = jnp.einsum('bqhd,bkhd->bhqk', q_b, k).astype(jnp.float32) * scale
        q_pos = q0 + jnp.arange(Q_BLOCK, dtype=jnp.int32)
        strict = key_pos[None, :] < q_pos[:, None]
        log_not = jnp.where(strict, jax.nn.log_sigmoid(-z), 0.0)
        later = lax.cumsum(log_not, axis=3, reverse=True) - log_not
        w = jnp.where(strict, jnp.exp(jax.nn.log_sigmoid(z) + later), 0.0)
        return jnp.einsum('bhqk,bkhd->bqhd', w.astype(v.dtype), v)

    out = lax.map(block, (qb, q0s))
    return out.transpose(1, 0, 2, 3, 4).reshape(B, S, H * Dh)


def token_mixer(h, cos, sin, w_in, q_norm_g, kv_norm_g, w_uq, w_ukv, w_branch_mla, w_branch_sb, w_out):
    B, S, _ = h.shape
    proj = h @ w_in
    cq, ckv, kr, sbq, sbk, sbv, gate_a, gate_b = jnp.split(proj, np.cumsum(IN_SIZES)[:-1].tolist(), axis=-1)
    q = (rmsnorm(cq, q_norm_g) @ w_uq).reshape(B, S, MLA_HEADS, MLA_NOPE + MLA_ROPE)
    q_nope, q_rope = q[..., :MLA_NOPE], q[..., MLA_NOPE:]
    q_rope = rope(q_rope, cos[:, :, None, :], sin[:, :, None, :])
    kv = (rmsnorm(ckv, kv_norm_g) @ w_ukv).reshape(B, S, MLA_HEADS, MLA_NOPE + MLA_V)
    k_nope, v_mla = kv[..., :MLA_NOPE], kv[..., MLA_NOPE:]
    k_rope = rope(kr, cos, sin)
    y_a = mla_attention(q_nope, q_rope, k_nope, k_rope, v_mla) @ w_branch_mla
    shp = (B, S, SB_HEADS, SB_DIM)
    y_b = stick_breaking_attention(sbq.reshape(shp), sbk.reshape(shp), sbv.reshape(shp)) @ w_branch_sb
    merged = jax.nn.sigmoid(gate_a) * y_a + jax.nn.sigmoid(gate_b) * y_b
    return merged @ w_out


def peer_ffn(h, w_q, sub_keys, u_tab, v_tab):
    B, S, D = h.shape
    q = (h @ w_q).reshape(B, S, PEER_HEADS, PEER_DKEY).astype(jnp.float32)
    half = PEER_DKEY // 2
    s1 = jnp.einsum('bshd,hnd->bshn', q[..., :half], sub_keys[:, 0].astype(jnp.float32))
    s2 = jnp.einsum('bshd,hnd->bshn', q[..., half:], sub_keys[:, 1].astype(jnp.float32))
    v1, i1 = lax.top_k(s1, PEER_TOPK)
    v2, i2 = lax.top_k(s2, PEER_TOPK)
    cand = (v1[..., :, None] + v2[..., None, :]).reshape(B, S, PEER_HEADS, PEER_TOPK * PEER_TOPK)
    sc, ci = lax.top_k(cand, PEER_TOPK)
    e1 = jnp.take_along_axis(i1, ci // PEER_TOPK, axis=-1)
    e2 = jnp.take_along_axis(i2, ci % PEER_TOPK, axis=-1)
    idx = e1 * PEER_NKEYS + e2
    g = jax.nn.softmax(sc, axis=-1).astype(h.dtype)
    nblk = (B * S) // PEER_TOK_BLOCK
    hb = h.reshape(nblk, PEER_TOK_BLOCK, D)
    ib = idx.reshape(nblk, PEER_TOK_BLOCK, PEER_HEADS * PEER_TOPK)
    gb = g.reshape(nblk, PEER_TOK_BLOCK, PEER_HEADS * PEER_TOPK)

    def block(args):
        h_b, i_b, g_b = args
        act = jax.nn.gelu(jnp.einsum('td,tkd->tk', h_b, u_tab[i_b]))
        return jnp.einsum('tk,tkd->td', g_b * act, v_tab[i_b])

    out = lax.map(block, (hb, ib, gb))
    return out.reshape(B, S, D)


def setup_inputs(seed: int = 0) -> dict:
    key = jax.random.key(seed)
    ks = jax.random.split(key, 20)

    def nrm(k, shape, scale):
        return jax.random.normal(k, shape, jnp.float32) * scale

    d_mla_out = MLA_HEADS * MLA_V
    return {
        "x": nrm(ks[0], (BATCH, SEQ, D_MODEL), 1.0),
        "c": nrm(ks[1], (BATCH, D_MODEL), 1.0),
        "positions": jax.random.randint(ks[2], (BATCH, 1), 0, 65536, dtype=jnp.int32)
                     + jnp.arange(SEQ, dtype=jnp.int32)[None, :],
        "ada_w": nrm(ks[3], (DEPTH, D_MODEL, N_ADA * D_MODEL), 0.5 * D_MODEL ** -0.5),
        "ada_b": nrm(ks[4], (DEPTH, N_ADA * D_MODEL), 0.02),
        "norm1_g": 1.0 + nrm(ks[5], (DEPTH, D_MODEL), 0.02),
        "norm2_g": 1.0 + nrm(ks[6], (DEPTH, D_MODEL), 0.02),
        "w_in": nrm(ks[7], (DEPTH, D_MODEL, IN_COLS), D_MODEL ** -0.5),
        "mla_q_norm_g": 1.0 + nrm(ks[8], (DEPTH, MLA_Q_RANK), 0.02),
        "mla_kv_norm_g": 1.0 + nrm(ks[9], (DEPTH, MLA_KV_RANK), 0.02),
        "mla_w_uq": nrm(ks[10], (DEPTH, MLA_Q_RANK, MLA_HEADS * (MLA_NOPE + MLA_ROPE)), MLA_Q_RANK ** -0.5),
        "mla_w_ukv": nrm(ks[11], (DEPTH, MLA_KV_RANK, MLA_HEADS * (MLA_NOPE + MLA_V)), MLA_KV_RANK ** -0.5),
        "w_branch_mla": nrm(ks[12], (DEPTH, d_mla_out, D_MODEL), d_mla_out ** -0.5),
        "w_branch_sb": nrm(ks[13], (DEPTH, SB_WIDTH, D_MODEL), SB_WIDTH ** -0.5),
        "w_out": nrm(ks[14], (DEPTH, D_MODEL, D_MODEL), D_MODEL ** -0.5),
        "peer_w_q": nrm(ks[15], (DEPTH, D_MODEL, PEER_HEADS * PEER_DKEY), D_MODEL ** -0.5),
        "peer_sub_keys": nrm(ks[16], (DEPTH, PEER_HEADS, 2, PEER_NKEYS, PEER_DKEY // 2), (PEER_DKEY // 2) ** -0.5),
        "peer_u": nrm(ks[17], (DEPTH, PEER_N, D_MODEL), D_MODEL ** -0.5),
        "peer_v": nrm(ks[18], (DEPTH, PEER_N, D_MODEL), 1.0),
        "final_norm_g": 1.0 + nrm(ks[19], (D_MODEL,), 0.02),
    }


def reference(x, c, positions, ada_w, ada_b, norm1_g, norm2_g, w_in, mla_q_norm_g, mla_kv_norm_g,
              mla_w_uq, mla_w_ukv, w_branch_mla, w_branch_sb, w_out, peer_w_q, peer_sub_keys,
              peer_u, peer_v, final_norm_g):
    inv_freq = ROPE_THETA ** (-jnp.arange(0, MLA_ROPE, 2, dtype=jnp.float32) / MLA_ROPE)
    ang = positions.astype(jnp.float32)[..., None] * inv_freq
    cos = jnp.cos(ang).astype(x.dtype)
    sin = jnp.sin(ang).astype(x.dtype)
    for l in range(DEPTH):
        mod = jax.nn.silu(c) @ ada_w[l] + ada_b[l]
        sh1, sc1, gt1, sh2, sc2, gt2 = [m[:, None, :] for m in jnp.split(mod, N_ADA, axis=-1)]
        h = rmsnorm(x, norm1_g[l]) * (1.0 + sc1) + sh1
        x = x + gt1 * token_mixer(h, cos, sin, w_in[l], mla_q_norm_g[l], mla_kv_norm_g[l],
                                  mla_w_uq[l], mla_w_ukv[l], w_branch_mla[l], w_branch_sb[l], w_out[l])
        h = rmsnorm(x, norm2_g[l]) * (1.0 + sc2) + sh2
        x = x + gt2 * peer_ffn(h, peer_w_q[l], peer_sub_keys[l], peer_u[l], peer_v[l])
    return rmsnorm(x, final_norm_g)
```

```python
import functools

import numpy as np
import jax
import jax.numpy as jnp
from jax import lax
from jax.experimental import pallas as pl
from jax.experimental.pallas import tpu as pltpu

F32 = jnp.float32
BF16 = jnp.bfloat16

D_MODEL = 2048
CHUNK = 64
EPS = 1e-6

MLA_HEADS = 8
MLA_Q_RANK = 512
MLA_KV_RANK = 256
MLA_NOPE = 128
MLA_ROPE = 64
MLA_V = 128
ROPE_THETA = 10000.0
MLA_QK_PAD = 256

SB_HEADS = 8
SB_DIM = 128
SB_WIDTH = SB_HEADS * SB_DIM

PEER_HEADS = 8
PEER_NKEYS = 128
PEER_N = PEER_NKEYS * PEER_NKEYS
PEER_DKEY = 256
PEER_TOPK = 16

N_ADA = 6

SMALL_W = 1024
COL_SB = SMALL_W
COL_GATE = SMALL_W + 3 * SB_WIDTH
IN_W = COL_GATE + 2 * D_MODEL

V7X_VMEM_LIMIT = 56 * 1024 * 1024
EXP_UNDERFLOW = -104.0

_NT = (((1,), (1,)), ((), ()))


def _cparams(sem, vmem=None):
    return pltpu.CompilerParams(dimension_semantics=sem, vmem_limit_bytes=vmem)


def _rms(x):
    return x * lax.rsqrt(jnp.mean(x * x, axis=-1, keepdims=True) + EPS)


def _ada_body(c_ref, w_ref, b_ref, o_ref):
    c = c_ref[...]
    a = (c * jax.nn.sigmoid(c)).astype(BF16)
    o_ref[...] = jnp.dot(a, w_ref[...].astype(BF16), preferred_element_type=F32) + b_ref[...]


def _ada(c, w, b):
    nb, d = c.shape
    n = w.shape[1]
    rows = 8
    cp = jnp.zeros((rows, d), F32).at[:nb].set(c)
    tn = 1024
    out = pl.pallas_call(
        _ada_body,
        grid=(n // tn,),
        in_specs=[pl.BlockSpec((rows, d), lambda j: (0, 0)),
                  pl.BlockSpec((d, tn), lambda j: (0, j)),
                  pl.BlockSpec((1, tn), lambda j: (0, j))],
        out_specs=pl.BlockSpec((rows, tn), lambda j: (0, j)),
        out_shape=jax.ShapeDtypeStruct((rows, n), F32),
        compiler_params=_cparams(("arbitrary",), V7X_VMEM_LIMIT),
        name="ada",
    )(cp, w, b.reshape(1, n))
    return out[:nb]


def _in_proj_body(x_ref, g_ref, sc_ref, sh_ref, w_ref, o_ref, h_scr):
    @pl.when(pl.program_id(1) == 0)
    def _():
        x = x_ref[...]
        h = (_rms(x) * g_ref[...]) * (1.0 + sc_ref[0]) + sh_ref[0]
        h_scr[...] = h.astype(BF16)

    o_ref[...] = jnp.dot(h_scr[...], w_ref[...], preferred_element_type=F32).astype(o_ref.dtype)


def _in_proj(x2, g, sc, sh, w, seq):
    t, d = x2.shape
    n = w.shape[1]
    tm = min(512, seq)
    tn = 1024
    per_b = seq // tm
    return pl.pallas_call(
        _in_proj_body,
        grid=(t // tm, n // tn),
        in_specs=[pl.BlockSpec((tm, d), lambda i, j: (i, 0)),
                  pl.BlockSpec((1, d), lambda i, j: (0, 0)),
                  pl.BlockSpec((1, 1, d), lambda i, j: (i // per_b, 0, 0)),
                  pl.BlockSpec((1, 1, d), lambda i, j: (i // per_b, 0, 0)),
                  pl.BlockSpec((d, tn), lambda i, j: (0, j))],
        out_specs=pl.BlockSpec((tm, tn), lambda i, j: (i, j)),
        out_shape=jax.ShapeDtypeStruct((t, n), BF16),
        scratch_shapes=[pltpu.VMEM((tm, d), BF16)],
        compiler_params=_cparams(("parallel", "arbitrary"), V7X_VMEM_LIMIT),
        name="in_proj",
    )(x2, g.reshape(1, d), sc, sh, w)


def _mla_prep_body(p_ref, pos_ref, invf_ref, gq_ref, gkv_ref, wq_ref, wqr_ref, wk_ref, wv_ref,
                   q_ref, k_ref, v_ref, *, scale):
    p = p_ref[...]
    cq = p[:, :MLA_Q_RANK].astype(F32)
    ckv = p[:, MLA_Q_RANK:MLA_Q_RANK + MLA_KV_RANK].astype(F32)
    kr = p[:, 768:896].astype(F32)
    krr = p[:, 896:1024].astype(F32)
    cqn = (_rms(cq) * gq_ref[...]).astype(BF16)
    ckvn = (_rms(ckv) * gkv_ref[...]).astype(BF16)

    ang = pos_ref[...] * invf_ref[...]
    live = lax.broadcasted_iota(jnp.int32, ang.shape, 1) < MLA_ROPE
    cs = jnp.where(live, jnp.cos(ang), 0.0)
    sn = jnp.where(live, jnp.sin(ang), 0.0)

    q = jnp.dot(cqn, wq_ref[...], preferred_element_type=F32)
    qr = jnp.dot(cqn, wqr_ref[...], preferred_element_type=F32)
    kn = jnp.dot(ckvn, wk_ref[...], preferred_element_type=F32)
    v_ref[...] = jnp.dot(ckvn, wv_ref[...], preferred_element_type=F32).astype(v_ref.dtype)
    krope = (kr * cs + krr * sn).astype(k_ref.dtype)
    for h in range(MLA_HEADS):
        a = h * MLA_QK_PAD
        b = a + MLA_NOPE
        c = a + MLA_QK_PAD
        q_ref[:, a:b] = (q[:, a:b] * scale).astype(q_ref.dtype)
        q_ref[:, b:c] = ((q[:, b:c] * cs + qr[:, h * 128:(h + 1) * 128] * sn) * scale).astype(q_ref.dtype)
        k_ref[:, a:b] = kn[:, h * 128:(h + 1) * 128].astype(k_ref.dtype)
        k_ref[:, b:c] = krope


def _mla_prep(proj, pos, invf, gq, gkv, wq, wqr, wk, wv):
    t = proj.shape[0]
    tm = min(512, t)
    hq = MLA_HEADS * MLA_QK_PAD
    hv = MLA_HEADS * MLA_V
    full = lambda a: pl.BlockSpec(a.shape, lambda i: (0,) * a.ndim)
    scale = float((MLA_NOPE + MLA_ROPE) ** -0.5)
    return pl.pallas_call(
        functools.partial(_mla_prep_body, scale=scale),
        grid=(t // tm,),
        in_specs=[pl.BlockSpec((tm, SMALL_W), lambda i: (i, 0)),
                  pl.BlockSpec((tm, 1), lambda i: (i, 0)),
                  full(invf), full(gq), full(gkv), full(wq), full(wqr), full(wk), full(wv)],
        out_specs=[pl.BlockSpec((tm, hq), lambda i: (i, 0)),
                   pl.BlockSpec((tm, hq), lambda i: (i, 0)),
                   pl.BlockSpec((tm, hv), lambda i: (i, 0))],
        out_shape=[jax.ShapeDtypeStruct((t, hq), BF16),
                   jax.ShapeDtypeStruct((t, hq), BF16),
                   jax.ShapeDtypeStruct((t, hv), BF16)],
        compiler_params=_cparams(("parallel",), V7X_VMEM_LIMIT),
        name="mla_prep",
    )(proj, pos, invf, gq, gkv, wq, wqr, wk, wv)


def _mla_attn_body(q_ref, k_ref, v_ref, o_ref, m_scr, l_scr, acc_scr, *, tq):
    i = pl.program_id(2)
    q = q_ref[...]
    m_scr[...] = jnp.full(m_scr.shape, -jnp.inf, F32)
    l_scr[...] = jnp.zeros(l_scr.shape, F32)
    acc_scr[...] = jnp.zeros(acc_scr.shape, F32)

    def step(j, diagonal):
        start = pl.multiple_of(j * tq, tq)
        k = k_ref[pl.ds(start, tq), :]
        v = v_ref[pl.ds(start, tq), :]
        s = lax.dot_general(q, k, _NT, preferred_element_type=F32)
        if diagonal:
            row = lax.broadcasted_iota(jnp.int32, s.shape, 0) // CHUNK
            col = lax.broadcasted_iota(jnp.int32, s.shape, 1) // CHUNK
            s = jnp.where(col <= row, s, -jnp.inf)
        m_prev = m_scr[...]
        m_new = jnp.maximum(m_prev, jnp.max(s, axis=-1, keepdims=True))
        p = jnp.exp(s - m_new)
        alpha = jnp.exp(m_prev - m_new)
        l_scr[...] = alpha * l_scr[...] + jnp.sum(p, axis=-1, keepdims=True)
        acc_scr[...] = alpha * acc_scr[...] + jnp.dot(p.astype(BF16), v, preferred_element_type=F32)
        m_scr[...] = m_new

    def body(j, carry):
        step(j, False)
        return carry

    lax.fori_loop(0, i, body, 0)
    step(i, True)
    o_ref[...] = (acc_scr[...] / l_scr[...]).astype(o_ref.dtype)


def _mla_attn(q, k, v, nb, seq):
    tq = min(512, seq)
    nq = seq // tq
    return pl.pallas_call(
        functools.partial(_mla_attn_body, tq=tq),
        grid=(nb, MLA_HEADS, nq),
        in_specs=[pl.BlockSpec((tq, MLA_QK_PAD), lambda b, h, i: (b * nq + i, h)),
                  pl.BlockSpec((seq, MLA_QK_PAD), lambda b, h, i: (b, h)),
                  pl.BlockSpec((seq, MLA_V), lambda b, h, i: (b, h))],
        out_specs=pl.BlockSpec((tq, MLA_V), lambda b, h, i: (b * nq + i, h)),
        out_shape=jax.ShapeDtypeStruct((nb * seq, MLA_HEADS * MLA_V), BF16),
        scratch_shapes=[pltpu.VMEM((tq, 1), F32), pltpu.VMEM((tq, 1), F32), pltpu.VMEM((tq, MLA_V), F32)],
        compiler_params=_cparams(("parallel", "parallel", "arbitrary"), V7X_VMEM_LIMIT),
        name="mla_attn",
    )(q, k, v)


def _sb_attn_body(q_ref, k_ref, v_ref, tri_ref, o_ref, r_scr, acc_scr, *, tq, scale):
    i = pl.program_id(2)
    q = q_ref[...]
    tri = tri_ref[...]
    r_scr[...] = jnp.zeros(r_scr.shape, F32)
    acc_scr[...] = jnp.zeros(acc_scr.shape, F32)

    def block(j, diagonal):
        start = pl.multiple_of(j * tq, tq)
        k = k_ref[pl.ds(start, tq), :]
        v = v_ref[pl.ds(start, tq), :]
        z = lax.dot_general(q, k, _NT, preferred_element_type=F32) * scale
        log_not = -(jnp.maximum(z, 0.0) + jnp.log1p(jnp.exp(-jnp.abs(z))))
        if diagonal:
            strict = (lax.broadcasted_iota(jnp.int32, z.shape, 1)
                      < lax.broadcasted_iota(jnp.int32, z.shape, 0))
            log_not = jnp.where(strict, log_not, 0.0)
        hi = log_not.astype(BF16)
        lo = (log_not - hi.astype(F32)).astype(BF16)
        later = (jnp.dot(hi, tri, preferred_element_type=F32)
                 + jnp.dot(lo, tri, preferred_element_type=F32) + r_scr[...])
        w = jnp.exp(z + log_not + later)
        if diagonal:
            w = jnp.where(strict, w, 0.0)
        acc_scr[...] += jnp.dot(w.astype(BF16), v, preferred_element_type=F32)
        r_scr[...] += jnp.sum(log_not, axis=-1, keepdims=True)

    block(i, True)

    def cond(carry):
        j, rmax = carry
        return jnp.logical_and(j >= 0, rmax > EXP_UNDERFLOW)

    def body(carry):
        j, _ = carry
        block(j, False)
        return j - 1, jnp.max(r_scr[...])

    lax.while_loop(cond, body, (i - 1, jnp.max(r_scr[...])))
    o_ref[...] = acc_scr[...].astype(o_ref.dtype)


def _sb_attn(proj, nb, seq):
    tq = min(256, seq)
    nq = seq // tq
    cb = COL_SB // SB_DIM
    tri = (jnp.arange(tq)[:, None] > jnp.arange(tq)[None, :]).astype(BF16)
    return pl.pallas_call(
        functools.partial(_sb_attn_body, tq=tq, scale=float(SB_DIM ** -0.5)),
        grid=(nb, SB_HEADS, nq),
        in_specs=[pl.BlockSpec((tq, SB_DIM), lambda b, h, i: (b * nq + i, cb + h)),
                  pl.BlockSpec((seq, SB_DIM), lambda b, h, i: (b, cb + SB_HEADS + h)),
                  pl.BlockSpec((seq, SB_DIM), lambda b, h, i: (b, cb + 2 * SB_HEADS + h)),
                  pl.BlockSpec((tq, tq), lambda b, h, i: (0, 0))],
        out_specs=pl.BlockSpec((tq, SB_DIM), lambda b, h, i: (b * nq + i, h)),
        out_shape=jax.ShapeDtypeStruct((nb * seq, SB_WIDTH), BF16),
        scratch_shapes=[pltpu.VMEM((tq, 1), F32), pltpu.VMEM((tq, SB_DIM), F32)],
        compiler_params=_cparams(("parallel", "parallel", "arbitrary"), V7X_VMEM_LIMIT),
        name="sb_attn",
    )(proj, proj, proj, tri)


def _mix_out_body(a_ref, b_ref, ga_ref, gb_ref, x_ref, wa_ref, wb_ref, wo_ref, gt_ref, g2_ref, sc_ref, sh_ref,
                  x1_ref, h2_ref):
    ya = jnp.dot(a_ref[...], wa_ref[...], preferred_element_type=F32)
    yb = jnp.dot(b_ref[...], wb_ref[...], preferred_element_type=F32)
    merged = (jax.nn.sigmoid(ga_ref[...].astype(F32)) * ya
              + jax.nn.sigmoid(gb_ref[...].astype(F32)) * yb)
    o = jnp.dot(merged.astype(BF16), wo_ref[...], preferred_element_type=F32)
    x1 = x_ref[...] + gt_ref[0] * o
    x1_ref[...] = x1
    h2 = (_rms(x1) * g2_ref[...]) * (1.0 + sc_ref[0]) + sh_ref[0]
    h2_ref[...] = h2.astype(h2_ref.dtype)


def _mix_out(att_a, att_b, proj, x2, wa, wb, wo, gt1, g2, sc2, sh2, seq):
    t, d = x2.shape
    tm = min(256, seq)
    per_b = seq // tm
    gcol = COL_GATE // d
    const = lambda a: pl.BlockSpec(a.shape, lambda i: (0,) * a.ndim, pipeline_mode=pl.Buffered(1))
    mod = lambda: pl.BlockSpec((1, 1, d), lambda i: (i // per_b, 0, 0))
    return pl.pallas_call(
        _mix_out_body,
        grid=(t // tm,),
        in_specs=[pl.BlockSpec((tm, att_a.shape[1]), lambda i: (i, 0)),
                  pl.BlockSpec((tm, att_b.shape[1]), lambda i: (i, 0)),
                  pl.BlockSpec((tm, d), lambda i: (i, gcol)),
                  pl.BlockSpec((tm, d), lambda i: (i, gcol + 1)),
                  pl.BlockSpec((tm, d), lambda i: (i, 0)),
                  const(wa), const(wb), const(wo), mod(),
                  pl.BlockSpec((1, d), lambda i: (0, 0)), mod(), mod()],
        out_specs=[pl.BlockSpec((tm, d), lambda i: (i, 0)),
                   pl.BlockSpec((tm, d), lambda i: (i, 0))],
        out_shape=[jax.ShapeDtypeStruct((t, d), F32), jax.ShapeDtypeStruct((t, d), BF16)],
        compiler_params=_cparams(("parallel",), V7X_VMEM_LIMIT),
        name="mix_out",
    )(att_a, att_b, proj, proj, x2, wa, wb, wo, gt1, g2.reshape(1, d), sc2, sh2)


def _split(a):
    hi = a.astype(BF16)
    return hi, (a - hi.astype(F32)).astype(BF16)


def _dot3_nt(a, b):
    ah, al = _split(a)
    bh, bl = _split(b)
    dg = lambda u, w: lax.dot_general(u, w, _NT, preferred_element_type=F32)
    return dg(ah, bh) + (dg(ah, bl) + dg(al, bh))


def _top_values(s, k):
    out = []
    cur = s
    for _ in range(k):
        mx = jnp.max(cur, axis=0, keepdims=True)
        out.append(mx)
        cur = jnp.where(cur == mx, -jnp.inf, cur)
    return jnp.concatenate(out, axis=0)


def _peer_score_body(h_ref, wq_ref, keys_ref, s1_ref, s2_ref, st_ref):
    q = jnp.dot(h_ref[...], wq_ref[...], preferred_element_type=F32)
    half = PEER_DKEY // 2
    for h in range(PEER_HEADS):
        base = h * PEER_DKEY
        s1 = _dot3_nt(keys_ref[h, 0], q[:, base:base + half])
        s2 = _dot3_nt(keys_ref[h, 1], q[:, base + half:base + PEER_DKEY])
        rows = slice(h * PEER_NKEYS, (h + 1) * PEER_NKEYS)
        s1_ref[rows, :] = s1
        s2_ref[rows, :] = s2
        v1 = _top_values(s1, PEER_TOPK)
        v2 = _top_values(s2, PEER_TOPK)
        cands = [v1[0:1] + v2]
        for a in range(1, PEER_TOPK):
            cands.append(v1[a:a + 1] + v2[:8])
        top = _top_values(jnp.concatenate(cands, axis=0), PEER_TOPK)
        m = top[0:1]
        z = jnp.sum(jnp.exp(top - m), axis=0, keepdims=True)
        tau = top[PEER_TOPK - 1:PEER_TOPK]
        st_ref[h * 8:(h + 1) * 8, :] = jnp.concatenate(
            [tau, m + jnp.log(z), jnp.zeros((6, tau.shape[1]), F32)], axis=0)


def _peer_score(h2, wq, keys):
    t, d = h2.shape
    tm = min(256, t)
    rows = PEER_HEADS * PEER_NKEYS
    return pl.pallas_call(
        _peer_score_body,
        grid=(t // tm,),
        in_specs=[pl.BlockSpec((tm, d), lambda i: (i, 0)),
                  pl.BlockSpec(wq.shape, lambda i: (0, 0)),
                  pl.BlockSpec(keys.shape, lambda i: (0, 0, 0, 0))],
        out_specs=[pl.BlockSpec((rows, tm), lambda i: (0, i)),
                   pl.BlockSpec((rows, tm), lambda i: (0, i)),
                   pl.BlockSpec((PEER_HEADS * 8, tm), lambda i: (0, i))],
        out_shape=[jax.ShapeDtypeStruct((rows, t), F32),
                   jax.ShapeDtypeStruct((rows, t), F32),
                   jax.ShapeDtypeStruct((PEER_HEADS * 8, t), F32)],
        compiler_params=_cparams(("parallel",), V7X_VMEM_LIMIT),
        name="peer_score",
    )(h2, wq, keys)


def _gelu_tanh(x):
    return 0.5 * x * (1.0 + jnp.tanh(0.7978845608028654 * (x + 0.044715 * (x * x * x))))


def _peer_dense_body(h_ref, u_ref, vt_ref, s1_ref, s2_ref, st_ref, o_ref, a_scr, *, groups):
    j = pl.program_id(1)

    @pl.when(j == 0)
    def _():
        o_ref[...] = jnp.zeros(o_ref.shape, o_ref.dtype)

    st = lax.dot_general(u_ref[...], h_ref[...], _NT, preferred_element_type=F32)
    for g in range(groups):
        e1 = j * groups + g
        w = jnp.zeros((PEER_NKEYS, st.shape[1]), F32)
        for h in range(PEER_HEADS):
            s = s1_ref[pl.ds(h * PEER_NKEYS + e1, 1), :] + s2_ref[h * PEER_NKEYS:(h + 1) * PEER_NKEYS, :]
            tau = st_ref[h * 8:h * 8 + 1, :]
            lse = st_ref[h * 8 + 1:h * 8 + 2, :]
            w = w + jnp.where(s >= tau, jnp.exp(s - lse), 0.0)
        blk = slice(g * PEER_NKEYS, (g + 1) * PEER_NKEYS)
        a_scr[blk, :] = (_gelu_tanh(st[blk, :]) * w).astype(a_scr.dtype)
    o_ref[...] += jnp.dot(vt_ref[...], a_scr[...], preferred_element_type=F32)


def _peer_dense(h2, u, vt, s1, s2, st):
    t, d = h2.shape
    ne = u.shape[0]
    tm = min(512, t)
    te = 512
    rows = PEER_HEADS * PEER_NKEYS
    return pl.pallas_call(
        functools.partial(_peer_dense_body, groups=te // PEER_NKEYS),
        grid=(t // tm, ne // te),
        in_specs=[pl.BlockSpec((tm, d), lambda i, j: (i, 0)),
                  pl.BlockSpec((te, d), lambda i, j: (j, 0)),
                  pl.BlockSpec((d, te), lambda i, j: (0, j)),
                  pl.BlockSpec((rows, tm), lambda i, j: (0, i)),
                  pl.BlockSpec((rows, tm), lambda i, j: (0, i)),
                  pl.BlockSpec((PEER_HEADS * 8, tm), lambda i, j: (0, i))],
        out_specs=pl.BlockSpec((d, tm), lambda i, j: (0, i)),
        out_shape=jax.ShapeDtypeStruct((d, t), F32),
        scratch_shapes=[pltpu.VMEM((te, tm), BF16)],
        compiler_params=_cparams(("parallel", "arbitrary"), V7X_VMEM_LIMIT),
        name="peer_dense",
    )(h2, u, vt, s1, s2, st)


def _final_body(x1_ref, pt_ref, gt_ref, g_ref, o_ref, *, last):
    x2 = x1_ref[...] + gt_ref[0] * pt_ref[...].T
    o_ref[...] = _rms(x2) * g_ref[...] if last else x2


def _final(x1, peer_t, gt2, g, seq, last):
    t, d = x1.shape
    tm = min(256, seq)
    per_b = seq // tm
    return pl.pallas_call(
        functools.partial(_final_body, last=last),
        grid=(t // tm,),
        in_specs=[pl.BlockSpec((tm, d), lambda i: (i, 0)),
                  pl.BlockSpec((d, tm), lambda i: (0, i)),
                  pl.BlockSpec((1, 1, d), lambda i: (i // per_b, 0, 0)),
                  pl.BlockSpec((1, d), lambda i: (0, 0))],
        out_specs=pl.BlockSpec((tm, d), lambda i: (i, 0)),
        out_shape=jax.ShapeDtypeStruct((t, d), F32),
        compiler_params=_cparams(("parallel",), V7X_VMEM_LIMIT),
        name="final",
    )(x1, peer_t, gt2, g.reshape(1, d))


def _rot_half_cols(w):
    half = MLA_ROPE // 2
    return jnp.concatenate([-w[..., half:], w[..., :half]], axis=-1)


def _prep_w_in(w_in):
    d = w_in.shape[0]
    o = 0
    cq = w_in[:, o:o + MLA_Q_RANK]; o += MLA_Q_RANK
    ckv = w_in[:, o:o + MLA_KV_RANK]; o += MLA_KV_RANK
    kr = w_in[:, o:o + MLA_ROPE]; o += MLA_ROPE
    rest = w_in[:, o:]
    pad = jnp.zeros((d, 128 - MLA_ROPE), w_in.dtype)
    return jnp.concatenate([cq, ckv, kr, pad, _rot_half_cols(kr), pad, rest], axis=1).astype(BF16)


def _prep_w_uq(w_uq):
    r = w_uq.shape[0]
    w = w_uq.reshape(r, MLA_HEADS, MLA_NOPE + MLA_ROPE)
    nope, rope = w[..., :MLA_NOPE], w[..., MLA_NOPE:]
    pad = jnp.zeros((r, MLA_HEADS, MLA_QK_PAD - MLA_NOPE - MLA_ROPE), w_uq.dtype)
    wq = jnp.concatenate([nope, rope, pad], axis=-1).reshape(r, MLA_HEADS * MLA_QK_PAD)
    wqr = jnp.concatenate([_rot_half_cols(rope), pad], axis=-1).reshape(r, MLA_HEADS * 128)
    return wq.astype(BF16), wqr.astype(BF16)


def _prep_w_ukv(w_ukv):
    r = w_ukv.shape[0]
    w = w_ukv.reshape(r, MLA_HEADS, MLA_NOPE + MLA_V)
    wk = w[..., :MLA_NOPE].reshape(r, MLA_HEADS * MLA_NOPE)
    wv = w[..., MLA_NOPE:].reshape(r, MLA_HEADS * MLA_V)
    return wk.astype(BF16), wv.astype(BF16)


def kernel(x, c, positions, ada_w, ada_b, norm1_g, norm2_g, w_in, mla_q_norm_g, mla_kv_norm_g, mla_w_uq, mla_w_ukv, w_branch_mla, w_branch_sb, w_out, peer_w_q, peer_sub_keys, peer_u, peer_v, final_norm_g):
    nb, seq, d = x.shape
    t = nb * seq
    x2 = x.reshape(t, d)
    depth = ada_w.shape[0]
    for l in range(depth):
        mod = _ada(c, ada_w[l], ada_b[l])
        sh1, sc1, gt1, sh2, sc2, gt2 = [m.reshape(nb, 1, d) for m in jnp.split(mod, N_ADA, axis=-1)]

        proj = _in_proj(x2, norm1_g[l], sc1, sh1, _prep_w_in(w_in[l]), seq)

        inv_freq = ROPE_THETA ** (-jnp.arange(0, MLA_ROPE, 2, dtype=F32) / MLA_ROPE)
        invf = jnp.concatenate([inv_freq, inv_freq, jnp.zeros((128 - MLA_ROPE,), F32)]).reshape(1, 128)
        pos = positions.astype(F32).reshape(t, 1)
        wq, wqr = _prep_w_uq(mla_w_uq[l])
        wk, wv = _prep_w_ukv(mla_w_ukv[l])
        q_a, k_a, v_a = _mla_prep(proj, pos, invf, mla_q_norm_g[l].reshape(1, -1), mla_kv_norm_g[l].reshape(1, -1),
                                  wq, wqr, wk, wv)
        att_a = _mla_attn(q_a, k_a, v_a, nb, seq)
        att_b = _sb_attn(proj, nb, seq)

        x2, h2 = _mix_out(att_a, att_b, proj, x2, w_branch_mla[l].astype(BF16), w_branch_sb[l].astype(BF16),
                          w_out[l].astype(BF16), gt1, norm2_g[l], sc2, sh2, seq)

        s1, s2, st = _peer_score(h2, peer_w_q[l].astype(BF16), peer_sub_keys[l])
        peer_t = _peer_dense(h2, peer_u[l].astype(BF16), peer_v[l].T.astype(BF16), s1, s2, st)
        x2 = _final(x2, peer_t, gt2, final_norm_g, seq, last=(l == depth - 1))
    return x2.reshape(nb, seq, d)
```

```python
import functools

import numpy as np
import jax
import jax.numpy as jnp
from jax import lax
from jax.experimental import pallas as pl
from jax.experimental.pallas import tpu as pltpu

F32 = jnp.float32
BF16 = jnp.bfloat16

D_MODEL = 2048
CHUNK = 64
EPS = 1e-6

MLA_HEADS = 8
MLA_Q_RANK = 512
MLA_KV_RANK = 256
MLA_NOPE = 128
MLA_ROPE = 64
MLA_V = 128
ROPE_THETA = 10000.0
MLA_QK_PAD = 256

SB_HEADS = 8
SB_DIM = 128
SB_WIDTH = SB_HEADS * SB_DIM

PEER_HEADS = 8
PEER_NKEYS = 128
PEER_N = PEER_NKEYS * PEER_NKEYS
PEER_DKEY = 256
PEER_TOPK = 16

N_ADA = 6

SMALL_W = 1024
COL_SB = SMALL_W
COL_GATE = SMALL_W + 3 * SB_WIDTH
IN_W = COL_GATE + 2 * D_MODEL

V7X_VMEM_LIMIT = 56 * 1024 * 1024
EXP_UNDERFLOW = -104.0

_NT = (((1,), (1,)), ((), ()))


def _cparams(sem, vmem=None):
    return pltpu.CompilerParams(dimension_semantics=sem, vmem_limit_bytes=vmem)


def _rms(x):
    return x * lax.rsqrt(jnp.mean(x * x, axis=-1, keepdims=True) + EPS)


def _ada_body(c_ref, w_ref, b_ref, o_ref):
    c = c_ref[...]
    a = (c * jax.nn.sigmoid(c)).astype(BF16)
    o_ref[...] = jnp.dot(a, w_ref[...].astype(BF16), preferred_element_type=F32) + b_ref[...]


def _ada(c, w, b):
    nb, d = c.shape
    n = w.shape[1]
    rows = 8
    cp = jnp.zeros((rows, d), F32).at[:nb].set(c)
    tn = 1024
    out = pl.pallas_call(
        _ada_body,
        grid=(n // tn,),
        in_specs=[pl.BlockSpec((rows, d), lambda j: (0, 0)),
                  pl.BlockSpec((d, tn), lambda j: (0, j)),
                  pl.BlockSpec((1, tn), lambda j: (0, j))],
        out_specs=pl.BlockSpec((rows, tn), lambda j: (0, j)),
        out_shape=jax.ShapeDtypeStruct((rows, n), F32),
        compiler_params=_cparams(("arbitrary",), V7X_VMEM_LIMIT),
        name="ada",
    )(cp, w, b.reshape(1, n))
    return out[:nb]


def _in_proj_body(x_ref, g_ref, sc_ref, sh_ref, w_ref, o_ref, h_scr):
    @pl.when(pl.program_id(1) == 0)
    def _():
        x = x_ref[...]
        h = (_rms(x) * g_ref[...]) * (1.0 + sc_ref[0]) + sh_ref[0]
        h_scr[...] = h.astype(BF16)

    o_ref[...] = jnp.dot(h_scr[...], w_ref[...], preferred_element_type=F32).astype(o_ref.dtype)


def _in_proj(x2, g, sc, sh, w, seq):
    t, d = x2.shape
    n = w.shape[1]
    tm = min(1024, seq)
    tn = 1024
    per_b = seq // tm
    return pl.pallas_call(
        _in_proj_body,
        grid=(t // tm, n // tn),
        in_specs=[pl.BlockSpec((tm, d), lambda i, j: (i, 0)),
                  pl.BlockSpec((1, d), lambda i, j: (0, 0)),
                  pl.BlockSpec((1, 1, d), lambda i, j: (i // per_b, 0, 0)),
                  pl.BlockSpec((1, 1, d), lambda i, j: (i // per_b, 0, 0)),
                  pl.BlockSpec((d, tn), lambda i, j: (0, j))],
        out_specs=pl.BlockSpec((tm, tn), lambda i, j: (i, j)),
        out_shape=jax.ShapeDtypeStruct((t, n), BF16),
        scratch_shapes=[pltpu.VMEM((tm, d), BF16)],
        compiler_params=_cparams(("parallel", "arbitrary"), V7X_VMEM_LIMIT),
        name="in_proj",
    )(x2, g.reshape(1, d), sc, sh, w)


def _mla_prep_body(p_ref, pos_ref, invf_ref, gq_ref, gkv_ref, wq_ref, wqr_ref, wk_ref, wvt_ref,
                   q_ref, k_ref, vt_ref, *, scale):
    p = p_ref[...]
    cq = p[:, :MLA_Q_RANK].astype(F32)
    ckv = p[:, MLA_Q_RANK:MLA_Q_RANK + MLA_KV_RANK].astype(F32)
    kr = p[:, 768:896].astype(F32)
    krr = p[:, 896:1024].astype(F32)
    cqn = (_rms(cq) * gq_ref[...]).astype(BF16)
    ckvn = (_rms(ckv) * gkv_ref[...]).astype(BF16)

    ang = pos_ref[...] * invf_ref[...]
    live = lax.broadcasted_iota(jnp.int32, ang.shape, 1) < MLA_ROPE
    cs = jnp.where(live, jnp.cos(ang), 0.0)
    sn = jnp.where(live, jnp.sin(ang), 0.0)

    q = jnp.dot(cqn, wq_ref[...], preferred_element_type=F32)
    qr = jnp.dot(cqn, wqr_ref[...], preferred_element_type=F32)
    kn = jnp.dot(ckvn, wk_ref[...], preferred_element_type=F32)
    vt_ref[0] = lax.dot_general(wvt_ref[...], ckvn, _NT, preferred_element_type=F32).astype(vt_ref.dtype)
    krope = (kr * cs + krr * sn).astype(k_ref.dtype)
    for h in range(MLA_HEADS):
        a = h * MLA_QK_PAD
        b = a + MLA_NOPE
        c = a + MLA_QK_PAD
        q_ref[:, a:b] = (q[:, a:b] * scale).astype(q_ref.dtype)
        q_ref[:, b:c] = ((q[:, b:c] * cs + qr[:, h * 128:(h + 1) * 128] * sn) * scale).astype(q_ref.dtype)
        k_ref[:, a:b] = kn[:, h * 128:(h + 1) * 128].astype(k_ref.dtype)
        k_ref[:, b:c] = krope


def _mla_prep(proj, pos, invf, gq, gkv, wq, wqr, wk, wvt, tm):
    t = proj.shape[0]
    hq = MLA_HEADS * MLA_QK_PAD
    hv = MLA_HEADS * MLA_V
    full = lambda a: pl.BlockSpec(a.shape, lambda i: (0,) * a.ndim)
    scale = float((MLA_NOPE + MLA_ROPE) ** -0.5)
    return pl.pallas_call(
        functools.partial(_mla_prep_body, scale=scale),
        grid=(t // tm,),
        in_specs=[pl.BlockSpec((tm, SMALL_W), lambda i: (i, 0)),
                  pl.BlockSpec((tm, 1), lambda i: (i, 0)),
                  full(invf), full(gq), full(gkv), full(wq), full(wqr), full(wk), full(wvt)],
        out_specs=[pl.BlockSpec((tm, hq), lambda i: (i, 0)),
                   pl.BlockSpec((tm, hq), lambda i: (i, 0)),
                   pl.BlockSpec((1, hv, tm), lambda i: (i, 0, 0))],
        out_shape=[jax.ShapeDtypeStruct((t, hq), BF16),
                   jax.ShapeDtypeStruct((t, hq), BF16),
                   jax.ShapeDtypeStruct((t // tm, hv, tm), BF16)],
        compiler_params=_cparams(("parallel",), V7X_VMEM_LIMIT),
        name="mla_prep",
    )(proj, pos, invf, gq, gkv, wq, wqr, wk, wvt)


ATTN_HEADS_PER_STEP = 2


def _mla_attn_body(q_ref, k_ref, vt_ref, o_ref, m_scr, l_scr, acc_scr, *, tq):
    i = pl.program_id(2)
    m_scr[...] = jnp.full(m_scr.shape, -jnp.inf, F32)
    l_scr[...] = jnp.zeros(l_scr.shape, F32)
    acc_scr[...] = jnp.zeros(acc_scr.shape, F32)

    def step(j, diagonal):
        start = pl.multiple_of(j * tq, tq)
        for hh in range(ATTN_HEADS_PER_STEP):
            cols = slice(hh * MLA_QK_PAD, (hh + 1) * MLA_QK_PAD)
            k = k_ref[pl.ds(start, tq), cols]
            s = lax.dot_general(k, q_ref[:, cols], _NT, preferred_element_type=F32)
            if diagonal:
                key_chunk = lax.broadcasted_iota(jnp.int32, s.shape, 0) // CHUNK
                q_chunk = lax.broadcasted_iota(jnp.int32, s.shape, 1) // CHUNK
                s = jnp.where(key_chunk <= q_chunk, s, -jnp.inf)
            m_prev = m_scr[hh]
            m_new = jnp.maximum(m_prev, jnp.max(s, axis=0, keepdims=True))
            p = jnp.exp(s - m_new)
            alpha = jnp.exp(m_prev - m_new)
            l_scr[hh] = alpha * l_scr[hh] + jnp.sum(p, axis=0, keepdims=True)
            vt = vt_ref[j, hh * MLA_V:(hh + 1) * MLA_V, :]
            acc_scr[hh] = alpha * acc_scr[hh] + jnp.dot(vt, p.astype(BF16), preferred_element_type=F32)
            m_scr[hh] = m_new

    def body(j, carry):
        step(j, False)
        return carry

    lax.fori_loop(0, i, body, 0)
    step(i, True)
    for hh in range(ATTN_HEADS_PER_STEP):
        o_ref[:, hh * MLA_V:(hh + 1) * MLA_V] = (acc_scr[hh] / l_scr[hh]).T.astype(o_ref.dtype)


def _mla_attn(q, k, vt, nb, seq, tq):
    nq = seq // tq
    hp = ATTN_HEADS_PER_STEP
    return pl.pallas_call(
        functools.partial(_mla_attn_body, tq=tq),
        grid=(nb, MLA_HEADS // hp, nq),
        in_specs=[pl.BlockSpec((tq, hp * MLA_QK_PAD), lambda b, h, i: (b * nq + i, h)),
                  pl.BlockSpec((seq, hp * MLA_QK_PAD), lambda b, h, i: (b, h)),
                  pl.BlockSpec((nq, hp * MLA_V, tq), lambda b, h, i: (b, h, 0))],
        out_specs=pl.BlockSpec((tq, hp * MLA_V), lambda b, h, i: (b * nq + i, h)),
        out_shape=jax.ShapeDtypeStruct((nb * seq, MLA_HEADS * MLA_V), BF16),
        scratch_shapes=[pltpu.VMEM((hp, 1, tq), F32), pltpu.VMEM((hp, 1, tq), F32),
                        pltpu.VMEM((hp, MLA_V, tq), F32)],
        compiler_params=_cparams(("parallel", "parallel", "arbitrary"), V7X_VMEM_LIMIT),
        name="mla_attn",
    )(q, k, vt)


def _sb_attn_body(q_ref, k_ref, v_ref, tri_ref, o_ref, r_scr, acc_scr, *, tq, scale):
    i = pl.program_id(2)
    q = q_ref[...]
    tri = tri_ref[...]
    r_scr[...] = jnp.zeros(r_scr.shape, F32)
    acc_scr[...] = jnp.zeros(acc_scr.shape, F32)

    def block(j, diagonal):
        start = pl.multiple_of(j * tq, tq)
        k = k_ref[pl.ds(start, tq), :]
        v = v_ref[pl.ds(start, tq), :]
        z = lax.dot_general(q, k, _NT, preferred_element_type=F32) * scale
        log_not = -(jnp.maximum(z, 0.0) + jnp.log1p(jnp.exp(-jnp.abs(z))))
        if diagonal:
            strict = (lax.broadcasted_iota(jnp.int32, z.shape, 1)
                      < lax.broadcasted_iota(jnp.int32, z.shape, 0))
            log_not = jnp.where(strict, log_not, 0.0)
        hi = log_not.astype(BF16)
        lo = (log_not - hi.astype(F32)).astype(BF16)
        later = (jnp.dot(hi, tri, preferred_element_type=F32)
                 + jnp.dot(lo, tri, preferred_element_type=F32) + r_scr[...])
        w = jnp.exp(z + log_not + later)
        if diagonal:
            w = jnp.where(strict, w, 0.0)
        acc_scr[...] += jnp.dot(w.astype(BF16), v, preferred_element_type=F32)
        r_scr[...] += jnp.sum(log_not, axis=-1, keepdims=True)

    block(i, True)

    def cond(carry):
        j, rmax = carry
        return jnp.logical_and(j >= 0, rmax > EXP_UNDERFLOW)

    def body(carry):
        j, _ = carry
        block(j, False)
        return j - 1, jnp.max(r_scr[...])

    lax.while_loop(cond, body, (i - 1, jnp.max(r_scr[...])))
    o_ref[...] = acc_scr[...].astype(o_ref.dtype)


def _sb_attn(proj, nb, seq):
    tq = min(256, seq)
    nq = seq // tq
    cb = COL_SB // SB_DIM
    tri = (jnp.arange(tq)[:, None] > jnp.arange(tq)[None, :]).astype(BF16)
    return pl.pallas_call(
        functools.partial(_sb_attn_body, tq=tq, scale=float(SB_DIM ** -0.5)),
        grid=(nb, SB_HEADS, nq),
        in_specs=[pl.BlockSpec((tq, SB_DIM), lambda b, h, i: (b * nq + i, cb + h)),
                  pl.BlockSpec((seq, SB_DIM), lambda b, h, i: (b, cb + SB_HEADS + h)),
                  pl.BlockSpec((seq, SB_DIM), lambda b, h, i: (b, cb + 2 * SB_HEADS + h)),
                  pl.BlockSpec((tq, tq), lambda b, h, i: (0, 0))],
        out_specs=pl.BlockSpec((tq, SB_DIM), lambda b, h, i: (b * nq + i, h)),
        out_shape=jax.ShapeDtypeStruct((nb * seq, SB_WIDTH), BF16),
        scratch_shapes=[pltpu.VMEM((tq, 1), F32), pltpu.VMEM((tq, SB_DIM), F32)],
        compiler_params=_cparams(("parallel", "parallel", "arbitrary"), V7X_VMEM_LIMIT),
        name="sb_attn",
    )(proj, proj, proj, tri)


def _mix_out_body(a_ref, b_ref, ga_ref, gb_ref, x_ref, wa_ref, wb_ref, wo_ref, gt_ref, g2_ref, sc_ref, sh_ref,
                  x1_ref, h2_ref):
    ya = jnp.dot(a_ref[...], wa_ref[...], preferred_element_type=F32)
    yb = jnp.dot(b_ref[...], wb_ref[...], preferred_element_type=F32)
    merged = (jax.nn.sigmoid(ga_ref[...].astype(F32)) * ya
              + jax.nn.sigmoid(gb_ref[...].astype(F32)) * yb)
    o = jnp.dot(merged.astype(BF16), wo_ref[...], preferred_element_type=F32)
    x1 = x_ref[...] + gt_ref[0] * o
    x1_ref[...] = x1
    h2 = (_rms(x1) * g2_ref[...]) * (1.0 + sc_ref[0]) + sh_ref[0]
    h2_ref[...] = h2.astype(h2_ref.dtype)


def _mix_out(att_a, att_b, proj, x2, wa, wb, wo, gt1, g2, sc2, sh2, seq):
    t, d = x2.shape
    tm = min(256, seq)
    per_b = seq // tm
    gcol = COL_GATE // d
    const = lambda a: pl.BlockSpec(a.shape, lambda i: (0,) * a.ndim, pipeline_mode=pl.Buffered(1))
    mod = lambda: pl.BlockSpec((1, 1, d), lambda i: (i // per_b, 0, 0))
    return pl.pallas_call(
        _mix_out_body,
        grid=(t // tm,),
        in_specs=[pl.BlockSpec((tm, att_a.shape[1]), lambda i: (i, 0)),
                  pl.BlockSpec((tm, att_b.shape[1]), lambda i: (i, 0)),
                  pl.BlockSpec((tm, d), lambda i: (i, gcol)),
                  pl.BlockSpec((tm, d), lambda i: (i, gcol + 1)),
                  pl.BlockSpec((tm, d), lambda i: (i, 0)),
                  const(wa), const(wb), const(wo), mod(),
                  pl.BlockSpec((1, d), lambda i: (0, 0)), mod(), mod()],
        out_specs=[pl.BlockSpec((tm, d), lambda i: (i, 0)),
                   pl.BlockSpec((tm, d), lambda i: (i, 0))],
        out_shape=[jax.ShapeDtypeStruct((t, d), F32), jax.ShapeDtypeStruct((t, d), BF16)],
        compiler_params=_cparams(("parallel",), V7X_VMEM_LIMIT),
        name="mix_out",
    )(att_a, att_b, proj, proj, x2, wa, wb, wo, gt1, g2.reshape(1, d), sc2, sh2)


def _split(a):
    hi = a.astype(BF16)
    return hi, (a - hi.astype(F32)).astype(BF16)


def _dot3_nt(a, b):
    ah, al = _split(a)
    bh, bl = _split(b)
    dg = lambda u, w: lax.dot_general(u, w, _NT, preferred_element_type=F32)
    return dg(ah, bh) + (dg(ah, bl) + dg(al, bh))


def _top_values(s, k):
    out = []
    cur = s
    for _ in range(k):
        mx = jnp.max(cur, axis=0, keepdims=True)
        out.append(mx)
        cur = jnp.where(cur == mx, -jnp.inf, cur)
    return jnp.concatenate(out, axis=0)


def _peer_score_body(h_ref, wq_ref, keys_ref, s1_ref, p1_ref, s2_ref, p2_ref, tau_ref):
    q = jnp.dot(h_ref[...], wq_ref[...], preferred_element_type=F32)
    half = PEER_DKEY // 2
    for h in range(PEER_HEADS):
        base = h * PEER_DKEY
        s1 = _dot3_nt(keys_ref[h, 0], q[:, base:base + half])
        s2 = _dot3_nt(keys_ref[h, 1], q[:, base + half:base + PEER_DKEY])
        v1 = _top_values(s1, PEER_TOPK)
        v2 = _top_values(s2, PEER_TOPK)
        cands = [v1[0:1] + v2]
        for a in range(1, PEER_TOPK):
            cands.append(v1[a:a + 1] + v2[:8])
        top = _top_values(jnp.concatenate(cands, axis=0), PEER_TOPK)
        z = jnp.sum(jnp.exp(top - top[0:1]), axis=0, keepdims=True)
        rows = slice(h * PEER_NKEYS, (h + 1) * PEER_NKEYS)
        s1_ref[rows, :] = s1
        s2_ref[rows, :] = s2
        p1_ref[rows, :] = jnp.exp(s1 - (v1[0:1] + jnp.log(z)))
        p2_ref[rows, :] = jnp.exp(s2 - v2[0:1])
        tau_ref[h * 8:(h + 1) * 8, :] = jnp.broadcast_to(top[PEER_TOPK - 1:PEER_TOPK], (8, s1.shape[1]))


def _peer_score(h2, wq, keys):
    t, d = h2.shape
    tm = min(256, t)
    rows = PEER_HEADS * PEER_NKEYS
    row_spec = pl.BlockSpec((rows, tm), lambda i: (0, i))
    row_shape = jax.ShapeDtypeStruct((rows, t), F32)
    return pl.pallas_call(
        _peer_score_body,
        grid=(t // tm,),
        in_specs=[pl.BlockSpec((tm, d), lambda i: (i, 0)),
                  pl.BlockSpec(wq.shape, lambda i: (0, 0)),
                  pl.BlockSpec(keys.shape, lambda i: (0, 0, 0, 0))],
        out_specs=[row_spec, row_spec, row_spec, row_spec,
                   pl.BlockSpec((PEER_HEADS * 8, tm), lambda i: (0, i))],
        out_shape=[row_shape, row_shape, row_shape, row_shape,
                   jax.ShapeDtypeStruct((PEER_HEADS * 8, t), F32)],
        compiler_params=_cparams(("parallel",), V7X_VMEM_LIMIT),
        name="peer_score",
    )(h2, wq, keys)


def _gelu_tanh(x):
    return 0.5 * x * (1.0 + jnp.tanh(0.7978845608028654 * (x + 0.044715 * (x * x * x))))


BUILD_ROWS = 32
MXU_PIECE = 512


def _peer_dense_body(h_ref, u_ref, vt_ref, s1_ref, p1_ref, s2_ref, p2_ref, tau_ref, o_ref,
                     a0_scr, a1_scr, st0_scr, st1_scr, s1b_scr, p1b_scr, *, groups):
    j = pl.program_id(1)

    @pl.when(j == 0)
    def _():
        o_ref[...] = jnp.zeros(o_ref.shape, o_ref.dtype)
        a1_scr[...] = jnp.zeros(a1_scr.shape, a1_scr.dtype)
        st1_scr[...] = jnp.zeros(st1_scr.shape, st1_scr.dtype)

    def run(a_prev, a_next, st_prev, st_next):
        tm = o_ref.shape[1]
        tile = jnp.clip(j - 1, 0, pl.num_programs(1) - 3)

        def values_piece(c):
            rows = slice(c * MXU_PIECE, (c + 1) * MXU_PIECE)
            o_ref[rows, :] += jnp.dot(vt_ref[rows, :], a_prev[...], preferred_element_type=F32)

        def scores_piece(c):
            ks = slice(c * MXU_PIECE, (c + 1) * MXU_PIECE)
            part = lax.dot_general(u_ref[:, ks], h_ref[:, ks], _NT, preferred_element_type=F32)
            if c == 0:
                st_next[...] = part
            else:
                st_next[...] += part

        def spread_rows(g):
            e1 = tile * groups + g
            for h in range(PEER_HEADS):
                row = pl.ds(h * PEER_NKEYS + e1, 1)
                dst = slice((g * PEER_HEADS + h) * 8, (g * PEER_HEADS + h + 1) * 8)
                s1b_scr[dst, :] = jnp.broadcast_to(s1_ref[row, :], (8, tm))
                p1b_scr[dst, :] = jnp.broadcast_to(p1_ref[row, :], (8, tm))

        def build_chunk(g, r0):
            subs = range(r0, r0 + BUILD_ROWS, 8)
            w = [None] * len(subs)
            for h in range(PEER_HEADS):
                src = slice((g * PEER_HEADS + h) * 8, (g * PEER_HEADS + h + 1) * 8)
                s1b = s1b_scr[src, :]
                p1b = p1b_scr[src, :]
                tau = tau_ref[h * 8:(h + 1) * 8, :]
                for n, r in enumerate(subs):
                    e2 = slice(h * PEER_NKEYS + r, h * PEER_NKEYS + r + 8)
                    term = jnp.where(s1b + s2_ref[e2, :] >= tau, p1b * p2_ref[e2, :], 0.0)
                    w[n] = term if w[n] is None else w[n] + term
            blk = slice(g * PEER_NKEYS + r0, g * PEER_NKEYS + r0 + BUILD_ROWS)
            a_next[blk, :] = (_gelu_tanh(st_prev[blk, :]) * jnp.concatenate(w, axis=0)).astype(a_next.dtype)

        d_model = o_ref.shape[0]
        pieces = []
        for c in range(d_model // MXU_PIECE):
            pieces += [functools.partial(values_piece, c), functools.partial(scores_piece, c)]
        chunks = []
        for g in range(groups):
            chunks.append(functools.partial(spread_rows, g))
            chunks += [functools.partial(build_chunk, g, r0) for r0 in range(0, PEER_NKEYS, BUILD_ROWS)]
        per_piece = -(-len(chunks) // len(pieces))
        for k, piece in enumerate(pieces):
            piece()
            for chunk in chunks[k * per_piece:(k + 1) * per_piece]:
                chunk()

    @pl.when(j % 2 == 0)
    def _():
        run(a1_scr, a0_scr, st1_scr, st0_scr)

    @pl.when(j % 2 == 1)
    def _():
        run(a0_scr, a1_scr, st0_scr, st1_scr)


def _peer_dense(h2, u, vt, s1, p1, s2, p2, tau):
    t, d = h2.shape
    ne = u.shape[0]
    tm = min(512, t)
    te = 512
    n_tiles = ne // te
    groups = te // PEER_NKEYS
    rows = PEER_HEADS * PEER_NKEYS
    row_spec = pl.BlockSpec((rows, tm), lambda i, j: (0, i))
    return pl.pallas_call(
        functools.partial(_peer_dense_body, groups=groups),
        grid=(t // tm, n_tiles + 2),
        in_specs=[pl.BlockSpec((tm, d), lambda i, j: (i, 0)),
                  pl.BlockSpec((te, d), lambda i, j: (jnp.minimum(j, n_tiles - 1), 0)),
                  pl.BlockSpec((d, te), lambda i, j: (0, jnp.clip(j - 2, 0, n_tiles - 1))),
                  row_spec, row_spec, row_spec, row_spec,
                  pl.BlockSpec((PEER_HEADS * 8, tm), lambda i, j: (0, i))],
        out_specs=pl.BlockSpec((d, tm), lambda i, j: (0, i)),
        out_shape=jax.ShapeDtypeStruct((d, t), F32),
        scratch_shapes=[pltpu.VMEM((te, tm), BF16), pltpu.VMEM((te, tm), BF16),
                        pltpu.VMEM((te, tm), F32), pltpu.VMEM((te, tm), F32),
                        pltpu.VMEM((te // PEER_NKEYS * PEER_HEADS * 8, tm), F32),
                        pltpu.VMEM((te // PEER_NKEYS * PEER_HEADS * 8, tm), F32)],
        compiler_params=pltpu.CompilerParams(
            dimension_semantics=("parallel", "arbitrary"), vmem_limit_bytes=V7X_VMEM_LIMIT,
        ),
        name="peer_dense",
    )(h2, u, vt, s1, p1, s2, p2, tau)


def _final_body(x1_ref, pt_ref, gt_ref, g_ref, o_ref, *, last):
    x2 = x1_ref[...] + gt_ref[0] * pt_ref[...].T
    o_ref[...] = _rms(x2) * g_ref[...] if last else x2


def _final(x1, peer_t, gt2, g, seq, last):
    t, d = x1.shape
    tm = min(256, seq)
    per_b = seq // tm
    return pl.pallas_call(
        functools.partial(_final_body, last=last),
        grid=(t // tm,),
        in_specs=[pl.BlockSpec((tm, d), lambda i: (i, 0)),
                  pl.BlockSpec((d, tm), lambda i: (0, i)),
                  pl.BlockSpec((1, 1, d), lambda i: (i // per_b, 0, 0)),
                  pl.BlockSpec((1, d), lambda i: (0, 0))],
        out_specs=pl.BlockSpec((tm, d), lambda i: (i, 0)),
        out_shape=jax.ShapeDtypeStruct((t, d), F32),
        compiler_params=_cparams(("parallel",), V7X_VMEM_LIMIT),
        name="final",
    )(x1, peer_t, gt2, g.reshape(1, d))


def _rot_half_cols(w):
    half = MLA_ROPE // 2
    return jnp.concatenate([-w[..., half:], w[..., :half]], axis=-1)


def _prep_w_in(w_in):
    d = w_in.shape[0]
    o = 0
    cq = w_in[:, o:o + MLA_Q_RANK]; o += MLA_Q_RANK
    ckv = w_in[:, o:o + MLA_KV_RANK]; o += MLA_KV_RANK
    kr = w_in[:, o:o + MLA_ROPE]; o += MLA_ROPE
    rest = w_in[:, o:]
    pad = jnp.zeros((d, 128 - MLA_ROPE), w_in.dtype)
    return jnp.concatenate([cq, ckv, kr, pad, _rot_half_cols(kr), pad, rest], axis=1).astype(BF16)


def _prep_w_uq(w_uq):
    r = w_uq.shape[0]
    w = w_uq.reshape(r, MLA_HEADS, MLA_NOPE + MLA_ROPE)
    nope, rope = w[..., :MLA_NOPE], w[..., MLA_NOPE:]
    pad = jnp.zeros((r, MLA_HEADS, MLA_QK_PAD - MLA_NOPE - MLA_ROPE), w_uq.dtype)
    wq = jnp.concatenate([nope, rope, pad], axis=-1).reshape(r, MLA_HEADS * MLA_QK_PAD)
    wqr = jnp.concatenate([_rot_half_cols(rope), pad], axis=-1).reshape(r, MLA_HEADS * 128)
    return wq.astype(BF16), wqr.astype(BF16)


def _prep_w_ukv(w_ukv):
    r = w_ukv.shape[0]
    w = w_ukv.reshape(r, MLA_HEADS, MLA_NOPE + MLA_V)
    wk = w[..., :MLA_NOPE].reshape(r, MLA_HEADS * MLA_NOPE)
    wv = w[..., MLA_NOPE:].reshape(r, MLA_HEADS * MLA_V)
    return wk.astype(BF16), wv.astype(BF16)


def kernel(x, c, positions, ada_w, ada_b, norm1_g, norm2_g, w_in, mla_q_norm_g, mla_kv_norm_g, mla_w_uq, mla_w_ukv, w_branch_mla, w_branch_sb, w_out, peer_w_q, peer_sub_keys, peer_u, peer_v, final_norm_g):
    nb, seq, d = x.shape
    t = nb * seq
    x2 = x.reshape(t, d)
    depth = ada_w.shape[0]
    for l in range(depth):
        mod = _ada(c, ada_w[l], ada_b[l])
        sh1, sc1, gt1, sh2, sc2, gt2 = [m.reshape(nb, 1, d) for m in jnp.split(mod, N_ADA, axis=-1)]

        proj = _in_proj(x2, norm1_g[l], sc1, sh1, _prep_w_in(w_in[l]), seq)

        inv_freq = ROPE_THETA ** (-jnp.arange(0, MLA_ROPE, 2, dtype=F32) / MLA_ROPE)
        invf = jnp.concatenate([inv_freq, inv_freq, jnp.zeros((128 - MLA_ROPE,), F32)]).reshape(1, 128)
        pos = positions.astype(F32).reshape(t, 1)
        wq, wqr = _prep_w_uq(mla_w_uq[l])
        wk, wv = _prep_w_ukv(mla_w_ukv[l])
        tq = min(512, seq)
        q_a, k_a, vt_a = _mla_prep(proj, pos, invf, mla_q_norm_g[l].reshape(1, -1), mla_kv_norm_g[l].reshape(1, -1),
                                   wq, wqr, wk, wv.T, tq)
        att_a = _mla_attn(q_a, k_a, vt_a, nb, seq, tq)
        att_b = _sb_attn(proj, nb, seq)

        x2, h2 = _mix_out(att_a, att_b, proj, x2, w_branch_mla[l].astype(BF16), w_branch_sb[l].astype(BF16),
                          w_out[l].astype(BF16), gt1, norm2_g[l], sc2, sh2, seq)

        s1, p1, s2, p2, tau = _peer_score(h2, peer_w_q[l].astype(BF16), peer_sub_keys[l])
        peer_t = _peer_dense(h2, peer_u[l].astype(BF16), peer_v[l].astype(BF16).T, s1, p1, s2, p2, tau)
        x2 = _final(x2, peer_t, gt2, final_norm_g, seq, last=(l == depth - 1))
    return x2.reshape(nb, seq, d)
```

```python
import functools

import numpy as np
import jax
import jax.numpy as jnp
from jax import lax
from jax.experimental import pallas as pl
from jax.experimental.pallas import tpu as pltpu

F32 = jnp.float32
BF16 = jnp.bfloat16

D_MODEL = 2048
CHUNK = 64
EPS = 1e-6

MLA_HEADS = 8
MLA_Q_RANK = 512
MLA_KV_RANK = 256
MLA_NOPE = 128
MLA_ROPE = 64
MLA_V = 128
ROPE_THETA = 10000.0
MLA_QK_PAD = 256

SB_HEADS = 8
SB_DIM = 128
SB_WIDTH = SB_HEADS * SB_DIM

PEER_HEADS = 8
PEER_NKEYS = 128
PEER_N = PEER_NKEYS * PEER_NKEYS
PEER_DKEY = 256
PEER_TOPK = 16

N_ADA = 6

SMALL_W = 1024
COL_SB = SMALL_W
COL_GATE = SMALL_W + 3 * SB_WIDTH
IN_W = COL_GATE + 2 * D_MODEL

V7X_VMEM_LIMIT = 56 * 1024 * 1024
EXP_UNDERFLOW = -104.0

_NT = (((1,), (1,)), ((), ()))


def _cparams(sem, vmem=None):
    return pltpu.CompilerParams(dimension_semantics=sem, vmem_limit_bytes=vmem)


def _rms(x):
    return x * lax.rsqrt(jnp.mean(x * x, axis=-1, keepdims=True) + EPS)


def _ada_body(c_ref, w_ref, b_ref, o_ref):
    c = c_ref[...]
    a = (c * jax.nn.sigmoid(c)).astype(BF16)
    o_ref[...] = jnp.dot(a, w_ref[...].astype(BF16), preferred_element_type=F32) + b_ref[...]


def _ada(c, w, b):
    nb, d = c.shape
    n = w.shape[1]
    rows = 8
    cp = jnp.zeros((rows, d), F32).at[:nb].set(c)
    tn = 1024
    out = pl.pallas_call(
        _ada_body,
        grid=(n // tn,),
        in_specs=[pl.BlockSpec((rows, d), lambda j: (0, 0)),
                  pl.BlockSpec((d, tn), lambda j: (0, j)),
                  pl.BlockSpec((1, tn), lambda j: (0, j))],
        out_specs=pl.BlockSpec((rows, tn), lambda j: (0, j)),
        out_shape=jax.ShapeDtypeStruct((rows, n), F32),
        compiler_params=_cparams(("arbitrary",), V7X_VMEM_LIMIT),
        name="ada",
    )(cp, w, b.reshape(1, n))
    return out[:nb]


def _in_proj_body(x_ref, g_ref, sc_ref, sh_ref, w_ref, o_ref, h_scr):
    @pl.when(pl.program_id(1) == 0)
    def _():
        x = x_ref[...]
        h = (_rms(x) * g_ref[...]) * (1.0 + sc_ref[0]) + sh_ref[0]
        h_scr[...] = h.astype(BF16)

    o_ref[...] = jnp.dot(h_scr[...], w_ref[...], preferred_element_type=F32).astype(o_ref.dtype)


def _in_proj(x2, g, sc, sh, w, seq):
    t, d = x2.shape
    n = w.shape[1]
    tm = min(1024, seq)
    tn = 1024
    per_b = seq // tm
    return pl.pallas_call(
        _in_proj_body,
        grid=(t // tm, n // tn),
        in_specs=[pl.BlockSpec((tm, d), lambda i, j: (i, 0)),
                  pl.BlockSpec((1, d), lambda i, j: (0, 0)),
                  pl.BlockSpec((1, 1, d), lambda i, j: (i // per_b, 0, 0)),
                  pl.BlockSpec((1, 1, d), lambda i, j: (i // per_b, 0, 0)),
                  pl.BlockSpec((d, tn), lambda i, j: (0, j))],
        out_specs=pl.BlockSpec((tm, tn), lambda i, j: (i, j)),
        out_shape=jax.ShapeDtypeStruct((t, n), BF16),
        scratch_shapes=[pltpu.VMEM((tm, d), BF16)],
        compiler_params=_cparams(("parallel", "arbitrary"), V7X_VMEM_LIMIT),
        name="in_proj",
    )(x2, g.reshape(1, d), sc, sh, w)


def _mla_prep_body(p_ref, pos_ref, invf_ref, gq_ref, gkv_ref, wq_ref, wqr_ref, wk_ref, wvt_ref,
                   q_ref, k_ref, vt_ref, *, scale):
    p = p_ref[...]
    cq = p[:, :MLA_Q_RANK].astype(F32)
    ckv = p[:, MLA_Q_RANK:MLA_Q_RANK + MLA_KV_RANK].astype(F32)
    kr = p[:, 768:896].astype(F32)
    krr = p[:, 896:1024].astype(F32)
    cqn = (_rms(cq) * gq_ref[...]).astype(BF16)
    ckvn = (_rms(ckv) * gkv_ref[...]).astype(BF16)

    ang = pos_ref[...] * invf_ref[...]
    live = lax.broadcasted_iota(jnp.int32, ang.shape, 1) < MLA_ROPE
    cs = jnp.where(live, jnp.cos(ang), 0.0)
    sn = jnp.where(live, jnp.sin(ang), 0.0)

    q = jnp.dot(cqn, wq_ref[...], preferred_element_type=F32)
    qr = jnp.dot(cqn, wqr_ref[...], preferred_element_type=F32)
    kn = jnp.dot(ckvn, wk_ref[...], preferred_element_type=F32)
    vt_ref[0] = lax.dot_general(wvt_ref[...], ckvn, _NT, preferred_element_type=F32).astype(vt_ref.dtype)
    krope = (kr * cs + krr * sn).astype(k_ref.dtype)
    for h in range(MLA_HEADS):
        a = h * MLA_QK_PAD
        b = a + MLA_NOPE
        c = a + MLA_QK_PAD
        q_ref[:, a:b] = (q[:, a:b] * scale).astype(q_ref.dtype)
        q_ref[:, b:c] = ((q[:, b:c] * cs + qr[:, h * 128:(h + 1) * 128] * sn) * scale).astype(q_ref.dtype)
        k_ref[:, a:b] = kn[:, h * 128:(h + 1) * 128].astype(k_ref.dtype)
        k_ref[:, b:c] = krope


def _mla_prep(proj, pos, invf, gq, gkv, wq, wqr, wk, wvt, tm):
    t = proj.shape[0]
    hq = MLA_HEADS * MLA_QK_PAD
    hv = MLA_HEADS * MLA_V
    full = lambda a: pl.BlockSpec(a.shape, lambda i: (0,) * a.ndim)
    scale = float((MLA_NOPE + MLA_ROPE) ** -0.5)
    return pl.pallas_call(
        functools.partial(_mla_prep_body, scale=scale),
        grid=(t // tm,),
        in_specs=[pl.BlockSpec((tm, SMALL_W), lambda i: (i, 0)),
                  pl.BlockSpec((tm, 1), lambda i: (i, 0)),
                  full(invf), full(gq), full(gkv), full(wq), full(wqr), full(wk), full(wvt)],
        out_specs=[pl.BlockSpec((tm, hq), lambda i: (i, 0)),
                   pl.BlockSpec((tm, hq), lambda i: (i, 0)),
                   pl.BlockSpec((1, hv, tm), lambda i: (i, 0, 0))],
        out_shape=[jax.ShapeDtypeStruct((t, hq), BF16),
                   jax.ShapeDtypeStruct((t, hq), BF16),
                   jax.ShapeDtypeStruct((t // tm, hv, tm), BF16)],
        compiler_params=_cparams(("parallel",), V7X_VMEM_LIMIT),
        name="mla_prep",
    )(proj, pos, invf, gq, gkv, wq, wqr, wk, wvt)


ATTN_HEADS_PER_STEP = 4


def _mla_attn_body(q_ref, k_ref, vt_ref, o_ref, m_scr, l_scr, acc_scr, *, tq):
    i = pl.program_id(2)
    m_scr[...] = jnp.full(m_scr.shape, -jnp.inf, F32)
    l_scr[...] = jnp.zeros(l_scr.shape, F32)
    acc_scr[...] = jnp.zeros(acc_scr.shape, F32)

    def step(j, diagonal):
        start = pl.multiple_of(j * tq, tq)
        heads = range(ATTN_HEADS_PER_STEP)
        cols = [slice(hh * MLA_QK_PAD, (hh + 1) * MLA_QK_PAD) for hh in heads]
        s = [lax.dot_general(k_ref[pl.ds(start, tq), cols[hh]], q_ref[:, cols[hh]], _NT,
                             preferred_element_type=F32) for hh in heads]
        if diagonal:
            key_chunk = lax.broadcasted_iota(jnp.int32, s[0].shape, 0) // CHUNK
            q_chunk = lax.broadcasted_iota(jnp.int32, s[0].shape, 1) // CHUNK
            s = [jnp.where(key_chunk <= q_chunk, x, -jnp.inf) for x in s]
        m_prev = [m_scr[hh] for hh in heads]
        m_new = [jnp.maximum(m_prev[hh], jnp.max(s[hh], axis=0, keepdims=True)) for hh in heads]
        p = [jnp.exp(s[hh] - m_new[hh]) for hh in heads]
        alpha = [jnp.exp(m_prev[hh] - m_new[hh]) for hh in heads]
        for hh in heads:
            l_scr[hh] = alpha[hh] * l_scr[hh] + jnp.sum(p[hh], axis=0, keepdims=True)
            vt = vt_ref[j, hh * MLA_V:(hh + 1) * MLA_V, :]
            acc_scr[hh] = alpha[hh] * acc_scr[hh] + jnp.dot(vt, p[hh].astype(BF16), preferred_element_type=F32)
            m_scr[hh] = m_new[hh]

    def body(j, carry):
        step(j, False)
        return carry

    lax.fori_loop(0, i, body, 0)
    step(i, True)
    for hh in range(ATTN_HEADS_PER_STEP):
        o_ref[:, hh * MLA_V:(hh + 1) * MLA_V] = (acc_scr[hh] / l_scr[hh]).T.astype(o_ref.dtype)


def _mla_attn(q, k, vt, nb, seq, tq):
    nq = seq // tq
    hp = ATTN_HEADS_PER_STEP
    return pl.pallas_call(
        functools.partial(_mla_attn_body, tq=tq),
        grid=(nb, MLA_HEADS // hp, nq),
        in_specs=[pl.BlockSpec((tq, hp * MLA_QK_PAD), lambda b, h, i: (b * nq + i, h)),
                  pl.BlockSpec((seq, hp * MLA_QK_PAD), lambda b, h, i: (b, h)),
                  pl.BlockSpec((nq, hp * MLA_V, tq), lambda b, h, i: (b, h, 0))],
        out_specs=pl.BlockSpec((tq, hp * MLA_V), lambda b, h, i: (b * nq + i, h)),
        out_shape=jax.ShapeDtypeStruct((nb * seq, MLA_HEADS * MLA_V), BF16),
        scratch_shapes=[pltpu.VMEM((hp, 1, tq), F32), pltpu.VMEM((hp, 1, tq), F32),
                        pltpu.VMEM((hp, MLA_V, tq), F32)],
        compiler_params=_cparams(("parallel", "parallel", "arbitrary"), V7X_VMEM_LIMIT),
        name="mla_attn",
    )(q, k, vt)


def _sb_attn_body(q_ref, k_ref, v_ref, tri_ref, o_ref, r_scr, acc_scr, *, tq, scale):
    i = pl.program_id(2)
    tri = tri_ref[...]
    r_scr[...] = jnp.zeros(r_scr.shape, F32)
    acc_scr[...] = jnp.zeros(acc_scr.shape, F32)

    def block(j, diagonal):
        start = pl.multiple_of(j * tq, tq)
        heads = range(ATTN_HEADS_PER_STEP)
        cols = [slice(hh * SB_DIM, (hh + 1) * SB_DIM) for hh in heads]
        z = [lax.dot_general(k_ref[pl.ds(start, tq), cols[hh]], q_ref[:, cols[hh]], _NT,
                             preferred_element_type=F32) * scale for hh in heads]
        log_not = [-(jnp.maximum(x, 0.0) + jnp.log1p(jnp.exp(-jnp.abs(x)))) for x in z]
        if diagonal:
            strict = (lax.broadcasted_iota(jnp.int32, z[0].shape, 0)
                      < lax.broadcasted_iota(jnp.int32, z[0].shape, 1))
            log_not = [jnp.where(strict, x, 0.0) for x in log_not]
        hi = [x.astype(BF16) for x in log_not]
        lo = [(x - h.astype(F32)).astype(BF16) for x, h in zip(log_not, hi)]
        later = [jnp.dot(tri, hi[hh], preferred_element_type=F32)
                 + jnp.dot(tri, lo[hh], preferred_element_type=F32) + r_scr[hh] for hh in heads]
        w = [jnp.exp(z[hh] + log_not[hh] + later[hh]) for hh in heads]
        if diagonal:
            w = [jnp.where(strict, x, 0.0) for x in w]
        for hh in heads:
            v = v_ref[pl.ds(start, tq), cols[hh]]
            acc_scr[hh] += jnp.dot(v.T, w[hh].astype(BF16), preferred_element_type=F32)
            r_scr[hh] += jnp.sum(log_not[hh], axis=0, keepdims=True)

    block(i, True)

    def cond(carry):
        j, rmax = carry
        return jnp.logical_and(j >= 0, rmax > EXP_UNDERFLOW)

    def body(carry):
        j, _ = carry
        block(j, False)
        return j - 1, jnp.max(r_scr[...])

    lax.while_loop(cond, body, (i - 1, jnp.max(r_scr[...])))
    for hh in range(ATTN_HEADS_PER_STEP):
        o_ref[:, hh * SB_DIM:(hh + 1) * SB_DIM] = acc_scr[hh].T.astype(o_ref.dtype)


def _sb_attn(proj, nb, seq):
    tq = min(256, seq)
    nq = seq // tq
    hp = ATTN_HEADS_PER_STEP
    cb = COL_SB // (hp * SB_DIM)
    hb = SB_HEADS // hp
    tri = (jnp.arange(tq)[None, :] > jnp.arange(tq)[:, None]).astype(BF16)
    return pl.pallas_call(
        functools.partial(_sb_attn_body, tq=tq, scale=float(SB_DIM ** -0.5)),
        grid=(nb, hb, nq),
        in_specs=[pl.BlockSpec((tq, hp * SB_DIM), lambda b, h, i: (b * nq + i, cb + h)),
                  pl.BlockSpec((seq, hp * SB_DIM), lambda b, h, i: (b, cb + hb + h)),
                  pl.BlockSpec((seq, hp * SB_DIM), lambda b, h, i: (b, cb + 2 * hb + h)),
                  pl.BlockSpec((tq, tq), lambda b, h, i: (0, 0))],
        out_specs=pl.BlockSpec((tq, hp * SB_DIM), lambda b, h, i: (b * nq + i, h)),
        out_shape=jax.ShapeDtypeStruct((nb * seq, SB_WIDTH), BF16),
        scratch_shapes=[pltpu.VMEM((hp, 1, tq), F32), pltpu.VMEM((hp, SB_DIM, tq), F32)],
        compiler_params=_cparams(("parallel", "parallel", "arbitrary"), V7X_VMEM_LIMIT),
        name="sb_attn",
    )(proj, proj, proj, tri)


def _mix_out_body(a_ref, b_ref, ga_ref, gb_ref, x_ref, wa_ref, wb_ref, wo_ref, gt_ref, g2_ref, sc_ref, sh_ref,
                  x1_ref, h2_ref):
    ya = jnp.dot(a_ref[...], wa_ref[...], preferred_element_type=F32)
    yb = jnp.dot(b_ref[...], wb_ref[...], preferred_element_type=F32)
    merged = (jax.nn.sigmoid(ga_ref[...].astype(F32)) * ya
              + jax.nn.sigmoid(gb_ref[...].astype(F32)) * yb)
    o = jnp.dot(merged.astype(BF16), wo_ref[...], preferred_element_type=F32)
    x1 = x_ref[...] + gt_ref[0] * o
    x1_ref[...] = x1
    h2 = (_rms(x1) * g2_ref[...]) * (1.0 + sc_ref[0]) + sh_ref[0]
    h2_ref[...] = h2.astype(h2_ref.dtype)


def _mix_out(att_a, att_b, proj, x2, wa, wb, wo, gt1, g2, sc2, sh2, seq):
    t, d = x2.shape
    tm = min(256, seq)
    per_b = seq // tm
    gcol = COL_GATE // d
    const = lambda a: pl.BlockSpec(a.shape, lambda i: (0,) * a.ndim, pipeline_mode=pl.Buffered(1))
    mod = lambda: pl.BlockSpec((1, 1, d), lambda i: (i // per_b, 0, 0))
    return pl.pallas_call(
        _mix_out_body,
        grid=(t // tm,),
        in_specs=[pl.BlockSpec((tm, att_a.shape[1]), lambda i: (i, 0)),
                  pl.BlockSpec((tm, att_b.shape[1]), lambda i: (i, 0)),
                  pl.BlockSpec((tm, d), lambda i: (i, gcol)),
                  pl.BlockSpec((tm, d), lambda i: (i, gcol + 1)),
                  pl.BlockSpec((tm, d), lambda i: (i, 0)),
                  const(wa), const(wb), const(wo), mod(),
                  pl.BlockSpec((1, d), lambda i: (0, 0)), mod(), mod()],
        out_specs=[pl.BlockSpec((tm, d), lambda i: (i, 0)),
                   pl.BlockSpec((tm, d), lambda i: (i, 0))],
        out_shape=[jax.ShapeDtypeStruct((t, d), F32), jax.ShapeDtypeStruct((t, d), BF16)],
        compiler_params=_cparams(("parallel",), V7X_VMEM_LIMIT),
        name="mix_out",
    )(att_a, att_b, proj, proj, x2, wa, wb, wo, gt1, g2.reshape(1, d), sc2, sh2)


def _split(a):
    hi = a.astype(BF16)
    return hi, (a - hi.astype(F32)).astype(BF16)


def _dot3_nt(a, b):
    ah, al = _split(a)
    bh, bl = _split(b)
    dg = lambda u, w: lax.dot_general(u, w, _NT, preferred_element_type=F32)
    return dg(ah, bh) + (dg(ah, bl) + dg(al, bh))


def _order(v, i, j):
    v[i], v[j] = jnp.maximum(v[i], v[j]), jnp.minimum(v[i], v[j])


def _bitonic_sort_desc(v):
    n = len(v)
    k = 2
    while k <= n:
        j = k // 2
        while j >= 1:
            for i in range(n):
                l = i ^ j
                if l > i:
                    if (i & k) == 0:
                        _order(v, i, l)
                    else:
                        _order(v, l, i)
            j //= 2
        k *= 2


def _bitonic_merge_desc(v):
    n = len(v)
    j = n // 2
    while j >= 1:
        for i in range(n):
            l = i ^ j
            if l > i:
                _order(v, i, l)
        j //= 2


def _merge_sublanes(v):
    n = len(v)
    for shift in (4, 2, 1):
        other = [pltpu.roll(x, shift, axis=0) for x in v]
        v = [jnp.maximum(v[k], other[n - 1 - k]) for k in range(n)]
        _bitonic_merge_desc(v)
    return v


def _top16_rows(s):
    v = [s[8 * k:8 * (k + 1), :] for k in range(s.shape[0] // 8)]
    _bitonic_sort_desc(v)
    return _merge_sublanes(v)


def _peer_score_body(h_ref, wq_ref, keys_ref, s1_ref, p1_ref, s2_ref, p2_ref, tau_ref, q0_scr, q1_scr):
    j = pl.program_id(0)

    @pl.when(j == 0)
    def _():
        q1_scr[...] = jnp.zeros(q1_scr.shape, q1_scr.dtype)

    @pl.when(j % 2 == 0)
    def _():
        _peer_rank(h_ref, wq_ref, q1_scr, q0_scr, keys_ref, s1_ref, p1_ref, s2_ref, p2_ref, tau_ref)

    @pl.when(j % 2 == 1)
    def _():
        _peer_rank(h_ref, wq_ref, q0_scr, q1_scr, keys_ref, s1_ref, p1_ref, s2_ref, p2_ref, tau_ref)


def _peer_rank(h_ref, wq_ref, q, q_next, keys_ref, s1_ref, p1_ref, s2_ref, p2_ref, tau_ref):
    half = PEER_DKEY // 2
    for h in range(PEER_HEADS):
        base = h * PEER_DKEY
        q_next[:, base:base + PEER_DKEY] = jnp.dot(h_ref[...], wq_ref[:, base:base + PEER_DKEY],
                                                   preferred_element_type=F32)
        s1 = _dot3_nt(keys_ref[h, 0], q[:, base:base + half])
        s2 = _dot3_nt(keys_ref[h, 1], q[:, base + half:base + PEER_DKEY])
        v1 = _top16_rows(s1)
        v2 = _top16_rows(s2)
        sub = lax.broadcasted_iota(jnp.int32, v2[0].shape, 0)
        v2_lo, v2_hi = v2[0], v2[8]
        for b in range(1, 8):
            v2_lo = jnp.where(sub == b, v2[b], v2_lo)
            v2_hi = jnp.where(sub == b, v2[8 + b], v2_hi)
        cand = [jnp.where(sub < PEER_TOPK // (a + 1), v1[a] + v2_lo, -jnp.inf) for a in range(PEER_TOPK)]
        extra = v1[0] + v2_hi
        for a in range(PEER_TOPK):
            cand[a], extra = jnp.maximum(cand[a], extra), jnp.minimum(cand[a], extra)
        top = _merge_sublanes(cand)
        z = functools.reduce(jnp.add, [jnp.exp(c - top[0]) for c in top])
        rows = slice(h * PEER_NKEYS, (h + 1) * PEER_NKEYS)
        s1_ref[rows, :] = s1
        s2_ref[rows, :] = s2
        p1_ref[rows, :] = jnp.exp(s1 - (v1[0] + jnp.log(z))[0:1, :])
        p2_ref[rows, :] = jnp.exp(s2 - v2[0][0:1, :])
        tau_ref[h * 8:(h + 1) * 8, :] = top[PEER_TOPK - 1]


def _peer_score(h2, wq, keys):
    t, d = h2.shape
    tm = min(256, t)
    n = t // tm
    rows = PEER_HEADS * PEER_NKEYS
    row_spec = pl.BlockSpec((rows, tm), lambda j: (0, jnp.maximum(j - 1, 0)))
    row_shape = jax.ShapeDtypeStruct((rows, t), F32)
    return pl.pallas_call(
        _peer_score_body,
        grid=(n + 1,),
        in_specs=[pl.BlockSpec((tm, d), lambda j: (jnp.minimum(j, n - 1), 0)),
                  pl.BlockSpec(wq.shape, lambda j: (0, 0)),
                  pl.BlockSpec(keys.shape, lambda j: (0, 0, 0, 0))],
        out_specs=[row_spec, row_spec, row_spec, row_spec,
                   pl.BlockSpec((PEER_HEADS * 8, tm), lambda j: (0, jnp.maximum(j - 1, 0)))],
        out_shape=[row_shape, row_shape, row_shape, row_shape,
                   jax.ShapeDtypeStruct((PEER_HEADS * 8, t), F32)],
        scratch_shapes=[pltpu.VMEM((tm, PEER_HEADS * PEER_DKEY), F32),
                        pltpu.VMEM((tm, PEER_HEADS * PEER_DKEY), F32)],
        compiler_params=_cparams(("arbitrary",), V7X_VMEM_LIMIT),
        name="peer_score",
    )(h2, wq, keys)


def _gelu_tanh(x):
    return 0.5 * x * (1.0 + jnp.tanh(0.7978845608028654 * (x + 0.044715 * (x * x * x))))


BUILD_ROWS = 32
MXU_PIECE = 512


def _peer_dense_body(h_ref, u_ref, vt_ref, s1_ref, p1_ref, s2_ref, p2_ref, tau_ref, o_ref,
                     a0_scr, a1_scr, st0_scr, st1_scr, s1b_scr, p1b_scr, *, groups):
    j = pl.program_id(1)

    @pl.when(j == 0)
    def _():
        o_ref[...] = jnp.zeros(o_ref.shape, o_ref.dtype)
        a1_scr[...] = jnp.zeros(a1_scr.shape, a1_scr.dtype)
        st1_scr[...] = jnp.zeros(st1_scr.shape, st1_scr.dtype)

    def run(a_prev, a_next, st_prev, st_next):
        tm = o_ref.shape[1]
        tile = jnp.clip(j - 1, 0, pl.num_programs(1) - 3)

        def values_piece(c):
            rows = slice(c * MXU_PIECE, (c + 1) * MXU_PIECE)
            o_ref[rows, :] += jnp.dot(vt_ref[rows, :], a_prev[...], preferred_element_type=F32)

        def scores_piece(c):
            ks = slice(c * MXU_PIECE, (c + 1) * MXU_PIECE)
            part = lax.dot_general(u_ref[:, ks], h_ref[:, ks], _NT, preferred_element_type=F32)
            if c == 0:
                st_next[...] = part
            else:
                st_next[...] += part

        def spread_rows(g):
            e1 = tile * groups + g
            for h in range(PEER_HEADS):
                row = pl.ds(h * PEER_NKEYS + e1, 1)
                dst = slice((g * PEER_HEADS + h) * 8, (g * PEER_HEADS + h + 1) * 8)
                s1b_scr[dst, :] = jnp.broadcast_to(s1_ref[row, :], (8, tm))
                p1b_scr[dst, :] = jnp.broadcast_to(p1_ref[row, :], (8, tm))

        def build_chunk(g, r0):
            subs = range(r0, r0 + BUILD_ROWS, 8)
            w = [None] * len(subs)
            for h in range(PEER_HEADS):
                src = slice((g * PEER_HEADS + h) * 8, (g * PEER_HEADS + h + 1) * 8)
                s1b = s1b_scr[src, :]
                p1b = p1b_scr[src, :]
                tau = tau_ref[h * 8:(h + 1) * 8, :]
                for n, r in enumerate(subs):
                    e2 = slice(h * PEER_NKEYS + r, h * PEER_NKEYS + r + 8)
                    term = jnp.where(s1b + s2_ref[e2, :] >= tau, p1b * p2_ref[e2, :], 0.0)
                    w[n] = term if w[n] is None else w[n] + term
            blk = slice(g * PEER_NKEYS + r0, g * PEER_NKEYS + r0 + BUILD_ROWS)
            a_next[blk, :] = (_gelu_tanh(st_prev[blk, :]) * jnp.concatenate(w, axis=0)).astype(a_next.dtype)

        d_model = o_ref.shape[0]
        pieces = []
        for c in range(d_model // MXU_PIECE):
            pieces += [functools.partial(values_piece, c), functools.partial(scores_piece, c)]
        chunks = []
        for g in range(groups):
            chunks.append(functools.partial(spread_rows, g))
            chunks += [functools.partial(build_chunk, g, r0) for r0 in range(0, PEER_NKEYS, BUILD_ROWS)]
        per_piece = -(-len(chunks) // len(pieces))
        for k, piece in enumerate(pieces):
            piece()
            for chunk in chunks[k * per_piece:(k + 1) * per_piece]:
                chunk()

    @pl.when(j % 2 == 0)
    def _():
        run(a1_scr, a0_scr, st1_scr, st0_scr)

    @pl.when(j % 2 == 1)
    def _():
        run(a0_scr, a1_scr, st0_scr, st1_scr)


def _peer_dense(h2, u, vt, s1, p1, s2, p2, tau):
    t, d = h2.shape
    ne = u.shape[0]
    tm = min(512, t)
    te = 512
    n_tiles = ne // te
    groups = te // PEER_NKEYS
    rows = PEER_HEADS * PEER_NKEYS
    row_spec = pl.BlockSpec((rows, tm), lambda i, j: (0, i))
    return pl.pallas_call(
        functools.partial(_peer_dense_body, groups=groups),
        grid=(t // tm, n_tiles + 2),
        in_specs=[pl.BlockSpec((tm, d), lambda i, j: (i, 0)),
                  pl.BlockSpec((te, d), lambda i, j: (jnp.minimum(j, n_tiles - 1), 0)),
                  pl.BlockSpec((d, te), lambda i, j: (0, jnp.clip(j - 2, 0, n_tiles - 1))),
                  row_spec, row_spec, row_spec, row_spec,
                  pl.BlockSpec((PEER_HEADS * 8, tm), lambda i, j: (0, i))],
        out_specs=pl.BlockSpec((d, tm), lambda i, j: (0, i)),
        out_shape=jax.ShapeDtypeStruct((d, t), F32),
        scratch_shapes=[pltpu.VMEM((te, tm), BF16), pltpu.VMEM((te, tm), BF16),
                        pltpu.VMEM((te, tm), F32), pltpu.VMEM((te, tm), F32),
                        pltpu.VMEM((te // PEER_NKEYS * PEER_HEADS * 8, tm), F32),
                        pltpu.VMEM((te // PEER_NKEYS * PEER_HEADS * 8, tm), F32)],
        compiler_params=pltpu.CompilerParams(
            dimension_semantics=("parallel", "arbitrary"), vmem_limit_bytes=V7X_VMEM_LIMIT,
        ),
        name="peer_dense",
    )(h2, u, vt, s1, p1, s2, p2, tau)


def _final_body(x1_ref, pt_ref, gt_ref, g_ref, o_ref, *, last):
    x2 = x1_ref[...] + gt_ref[0] * pt_ref[...].T
    o_ref[...] = _rms(x2) * g_ref[...] if last else x2


def _final(x1, peer_t, gt2, g, seq, last):
    t, d = x1.shape
    tm = min(256, seq)
    per_b = seq // tm
    return pl.pallas_call(
        functools.partial(_final_body, last=last),
        grid=(t // tm,),
        in_specs=[pl.BlockSpec((tm, d), lambda i: (i, 0)),
                  pl.BlockSpec((d, tm), lambda i: (0, i)),
                  pl.BlockSpec((1, 1, d), lambda i: (i // per_b, 0, 0)),
                  pl.BlockSpec((1, d), lambda i: (0, 0))],
        out_specs=pl.BlockSpec((tm, d), lambda i: (i, 0)),
        out_shape=jax.ShapeDtypeStruct((t, d), F32),
        compiler_params=_cparams(("parallel",), V7X_VMEM_LIMIT),
        name="final",
    )(x1, peer_t, gt2, g.reshape(1, d))


def _rot_half_cols(w):
    half = MLA_ROPE // 2
    return jnp.concatenate([-w[..., half:], w[..., :half]], axis=-1)


def _prep_w_in(w_in):
    d = w_in.shape[0]
    o = 0
    cq = w_in[:, o:o + MLA_Q_RANK]; o += MLA_Q_RANK
    ckv = w_in[:, o:o + MLA_KV_RANK]; o += MLA_KV_RANK
    kr = w_in[:, o:o + MLA_ROPE]; o += MLA_ROPE
    rest = w_in[:, o:]
    pad = jnp.zeros((d, 128 - MLA_ROPE), w_in.dtype)
    return jnp.concatenate([cq, ckv, kr, pad, _rot_half_cols(kr), pad, rest], axis=1).astype(BF16)


def _prep_w_uq(w_uq):
    r = w_uq.shape[0]
    w = w_uq.reshape(r, MLA_HEADS, MLA_NOPE + MLA_ROPE)
    nope, rope = w[..., :MLA_NOPE], w[..., MLA_NOPE:]
    pad = jnp.zeros((r, MLA_HEADS, MLA_QK_PAD - MLA_NOPE - MLA_ROPE), w_uq.dtype)
    wq = jnp.concatenate([nope, rope, pad], axis=-1).reshape(r, MLA_HEADS * MLA_QK_PAD)
    wqr = jnp.concatenate([_rot_half_cols(rope), pad], axis=-1).reshape(r, MLA_HEADS * 128)
    return wq.astype(BF16), wqr.astype(BF16)


def _prep_w_ukv(w_ukv):
    r = w_ukv.shape[0]
    w = w_ukv.reshape(r, MLA_HEADS, MLA_NOPE + MLA_V)
    wk = w[..., :MLA_NOPE].reshape(r, MLA_HEADS * MLA_NOPE)
    wv = w[..., MLA_NOPE:].reshape(r, MLA_HEADS * MLA_V)
    return wk.astype(BF16), wv.astype(BF16)


def kernel(x, c, positions, ada_w, ada_b, norm1_g, norm2_g, w_in, mla_q_norm_g, mla_kv_norm_g, mla_w_uq, mla_w_ukv, w_branch_mla, w_branch_sb, w_out, peer_w_q, peer_sub_keys, peer_u, peer_v, final_norm_g):
    nb, seq, d = x.shape
    t = nb * seq
    x2 = x.reshape(t, d)
    depth = ada_w.shape[0]
    for l in range(depth):
        mod = _ada(c, ada_w[l], ada_b[l])
        sh1, sc1, gt1, sh2, sc2, gt2 = [m.reshape(nb, 1, d) for m in jnp.split(mod, N_ADA, axis=-1)]

        proj = _in_proj(x2, norm1_g[l], sc1, sh1, _prep_w_in(w_in[l]), seq)

        inv_freq = ROPE_THETA ** (-jnp.arange(0, MLA_ROPE, 2, dtype=F32) / MLA_ROPE)
        invf = jnp.concatenate([inv_freq, inv_freq, jnp.zeros((128 - MLA_ROPE,), F32)]).reshape(1, 128)
        pos = positions.astype(F32).reshape(t, 1)
        wq, wqr = _prep_w_uq(mla_w_uq[l])
        wk, wv = _prep_w_ukv(mla_w_ukv[l])
        tq = min(512, seq)
        q_a, k_a, vt_a = _mla_prep(proj, pos, invf, mla_q_norm_g[l].reshape(1, -1), mla_kv_norm_g[l].reshape(1, -1),
                                   wq, wqr, wk, wv.T, tq)
        att_a = _mla_attn(q_a, k_a, vt_a, nb, seq, tq)
        att_b = _sb_attn(proj, nb, seq)

        x2, h2 = _mix_out(att_a, att_b, proj, x2, w_branch_mla[l].astype(BF16), w_branch_sb[l].astype(BF16),
                          w_out[l].astype(BF16), gt1, norm2_g[l], sc2, sh2, seq)

        s1, p1, s2, p2, tau = _peer_score(h2, peer_w_q[l].astype(BF16), peer_sub_keys[l])
        peer_t = _peer_dense(h2, peer_u[l].astype(BF16), peer_v[l].astype(BF16).T, s1, p1, s2, p2, tau)
        x2 = _final(x2, peer_t, gt2, final_norm_g, seq, last=(l == depth - 1))
    return x2.reshape(nb, seq, d)
```

```python
import functools

import numpy as np
import jax
import jax.numpy as jnp
from jax import lax
from jax.experimental import pallas as pl
from jax.experimental.pallas import tpu as pltpu

F32 = jnp.float32
BF16 = jnp.bfloat16

D_MODEL = 2048
CHUNK = 64
EPS = 1e-6

MLA_HEADS = 8
MLA_Q_RANK = 512
MLA_KV_RANK = 256
MLA_NOPE = 128
MLA_ROPE = 64
MLA_V = 128
ROPE_THETA = 10000.0
MLA_QK_PAD = 256

SB_HEADS = 8
SB_DIM = 128
SB_WIDTH = SB_HEADS * SB_DIM

PEER_HEADS = 8
PEER_NKEYS = 128
PEER_N = PEER_NKEYS * PEER_NKEYS
PEER_DKEY = 256
PEER_TOPK = 16

N_ADA = 6

SMALL_W = 1024
COL_SB = SMALL_W
COL_GATE = SMALL_W + 3 * SB_WIDTH
IN_W = COL_GATE + 2 * D_MODEL

V7X_VMEM_LIMIT = 60000 * 1024
EXP_UNDERFLOW = -104.0

_NT = (((1,), (1,)), ((), ()))


def _cparams(sem, vmem=None):
    return pltpu.CompilerParams(dimension_semantics=sem, vmem_limit_bytes=vmem)


def _rms(x):
    return x * lax.rsqrt(jnp.mean(x * x, axis=-1, keepdims=True) + EPS)


def _ada_body(c_ref, w_ref, b_ref, o_ref):
    c = c_ref[...]
    a = (c * jax.nn.sigmoid(c)).astype(BF16)
    o_ref[...] = jnp.dot(a, w_ref[...].astype(BF16), preferred_element_type=F32) + b_ref[...]


def _ada(c, w, b):
    nb, d = c.shape
    n = w.shape[1]
    rows = 8
    cp = jnp.zeros((rows, d), F32).at[:nb].set(c)
    tn = 1024
    out = pl.pallas_call(
        _ada_body,
        grid=(n // tn,),
        in_specs=[pl.BlockSpec((rows, d), lambda j: (0, 0)),
                  pl.BlockSpec((d, tn), lambda j: (0, j)),
                  pl.BlockSpec((1, tn), lambda j: (0, j))],
        out_specs=pl.BlockSpec((rows, tn), lambda j: (0, j)),
        out_shape=jax.ShapeDtypeStruct((rows, n), F32),
        compiler_params=_cparams(("arbitrary",), V7X_VMEM_LIMIT),
        name="ada",
    )(cp, w, b.reshape(1, n))
    return out[:nb]


def _in_proj_body(x_ref, g_ref, sc_ref, sh_ref, w_ref, o_ref, h_scr):
    @pl.when(pl.program_id(1) == 0)
    def _():
        x = x_ref[...]
        h = (_rms(x) * g_ref[...]) * (1.0 + sc_ref[0]) + sh_ref[0]
        h_scr[...] = h.astype(BF16)

    o_ref[...] = jnp.dot(h_scr[...], w_ref[...], preferred_element_type=F32).astype(o_ref.dtype)


def _in_proj(x2, g, sc, sh, w, seq):
    t, d = x2.shape
    n = w.shape[1]
    tm = min(1024, seq)
    tn = 1024
    per_b = seq // tm
    return pl.pallas_call(
        _in_proj_body,
        grid=(t // tm, n // tn),
        in_specs=[pl.BlockSpec((tm, d), lambda i, j: (i, 0)),
                  pl.BlockSpec((1, d), lambda i, j: (0, 0)),
                  pl.BlockSpec((1, 1, d), lambda i, j: (i // per_b, 0, 0)),
                  pl.BlockSpec((1, 1, d), lambda i, j: (i // per_b, 0, 0)),
                  pl.BlockSpec((d, tn), lambda i, j: (0, j))],
        out_specs=pl.BlockSpec((tm, tn), lambda i, j: (i, j)),
        out_shape=jax.ShapeDtypeStruct((t, n), BF16),
        scratch_shapes=[pltpu.VMEM((tm, d), BF16)],
        compiler_params=_cparams(("parallel", "arbitrary"), V7X_VMEM_LIMIT),
        name="in_proj",
    )(x2, g.reshape(1, d), sc, sh, w)


def _mla_prep_body(p_ref, pos_ref, invf_ref, gq_ref, gkv_ref, wq_ref, wqr_ref, wk_ref, wvt_ref,
                   q_ref, k_ref, vt_ref, *, scale):
    p = p_ref[...]
    cq = p[:, :MLA_Q_RANK].astype(F32)
    ckv = p[:, MLA_Q_RANK:MLA_Q_RANK + MLA_KV_RANK].astype(F32)
    kr = p[:, 768:896].astype(F32)
    krr = p[:, 896:1024].astype(F32)
    cqn = (_rms(cq) * gq_ref[...]).astype(BF16)
    ckvn = (_rms(ckv) * gkv_ref[...]).astype(BF16)

    ang = pos_ref[...] * invf_ref[...]
    live = lax.broadcasted_iota(jnp.int32, ang.shape, 1) < MLA_ROPE
    cs = jnp.where(live, jnp.cos(ang), 0.0)
    sn = jnp.where(live, jnp.sin(ang), 0.0)

    q = jnp.dot(cqn, wq_ref[...], preferred_element_type=F32)
    qr = jnp.dot(cqn, wqr_ref[...], preferred_element_type=F32)
    kn = jnp.dot(ckvn, wk_ref[...], preferred_element_type=F32)
    vt_ref[0] = lax.dot_general(wvt_ref[...], ckvn, _NT, preferred_element_type=F32).astype(vt_ref.dtype)
    krope = (kr * cs + krr * sn).astype(k_ref.dtype)
    for h in range(MLA_HEADS):
        a = h * MLA_QK_PAD
        b = a + MLA_NOPE
        c = a + MLA_QK_PAD
        q_ref[:, a:b] = (q[:, a:b] * scale).astype(q_ref.dtype)
        q_ref[:, b:c] = ((q[:, b:c] * cs + qr[:, h * 128:(h + 1) * 128] * sn) * scale).astype(q_ref.dtype)
        k_ref[:, a:b] = kn[:, h * 128:(h + 1) * 128].astype(k_ref.dtype)
        k_ref[:, b:c] = krope


def _mla_prep(proj, pos, invf, gq, gkv, wq, wqr, wk, wvt, tm):
    t = proj.shape[0]
    hq = MLA_HEADS * MLA_QK_PAD
    hv = MLA_HEADS * MLA_V
    full = lambda a: pl.BlockSpec(a.shape, lambda i: (0,) * a.ndim)
    scale = float((MLA_NOPE + MLA_ROPE) ** -0.5)
    return pl.pallas_call(
        functools.partial(_mla_prep_body, scale=scale),
        grid=(t // tm,),
        in_specs=[pl.BlockSpec((tm, SMALL_W), lambda i: (i, 0)),
                  pl.BlockSpec((tm, 1), lambda i: (i, 0)),
                  full(invf), full(gq), full(gkv), full(wq), full(wqr), full(wk), full(wvt)],
        out_specs=[pl.BlockSpec((tm, hq), lambda i: (i, 0)),
                   pl.BlockSpec((tm, hq), lambda i: (i, 0)),
                   pl.BlockSpec((1, hv, tm), lambda i: (i, 0, 0))],
        out_shape=[jax.ShapeDtypeStruct((t, hq), BF16),
                   jax.ShapeDtypeStruct((t, hq), BF16),
                   jax.ShapeDtypeStruct((t // tm, hv, tm), BF16)],
        compiler_params=_cparams(("parallel",), V7X_VMEM_LIMIT),
        name="mla_prep",
    )(proj, pos, invf, gq, gkv, wq, wqr, wk, wvt)


ATTN_HEADS_PER_STEP = 4


def _mla_attn_body(q_ref, k_ref, vt_ref, o_ref, m_scr, l_scr, acc_scr, *, tq):
    i = pl.program_id(2)
    m_scr[...] = jnp.full(m_scr.shape, -jnp.inf, F32)
    l_scr[...] = jnp.zeros(l_scr.shape, F32)
    acc_scr[...] = jnp.zeros(acc_scr.shape, F32)

    def step(j, diagonal):
        start = pl.multiple_of(j * tq, tq)
        heads = range(ATTN_HEADS_PER_STEP)
        cols = [slice(hh * MLA_QK_PAD, (hh + 1) * MLA_QK_PAD) for hh in heads]
        s = [lax.dot_general(k_ref[pl.ds(start, tq), cols[hh]], q_ref[:, cols[hh]], _NT,
                             preferred_element_type=F32) for hh in heads]
        if diagonal:
            key_chunk = lax.broadcasted_iota(jnp.int32, s[0].shape, 0) // CHUNK
            q_chunk = lax.broadcasted_iota(jnp.int32, s[0].shape, 1) // CHUNK
            s = [jnp.where(key_chunk <= q_chunk, x, -jnp.inf) for x in s]
        m_prev = [m_scr[hh] for hh in heads]
        m_new = [jnp.maximum(m_prev[hh], jnp.max(s[hh], axis=0, keepdims=True)) for hh in heads]
        p = [jnp.exp(s[hh] - m_new[hh]) for hh in heads]
        alpha = [jnp.exp(m_prev[hh] - m_new[hh]) for hh in heads]
        for hh in heads:
            l_scr[hh] = alpha[hh] * l_scr[hh] + jnp.sum(p[hh], axis=0, keepdims=True)
            vt = vt_ref[j, hh * MLA_V:(hh + 1) * MLA_V, :]
            acc_scr[hh] = alpha[hh] * acc_scr[hh] + jnp.dot(vt, p[hh].astype(BF16), preferred_element_type=F32)
            m_scr[hh] = m_new[hh]

    def body(j, carry):
        step(j, False)
        return carry

    lax.fori_loop(0, i, body, 0)
    step(i, True)
    for hh in range(ATTN_HEADS_PER_STEP):
        o_ref[:, hh * MLA_V:(hh + 1) * MLA_V] = (acc_scr[hh] / l_scr[hh]).T.astype(o_ref.dtype)


def _mla_attn(q, k, vt, nb, seq, tq):
    nq = seq // tq
    hp = ATTN_HEADS_PER_STEP
    return pl.pallas_call(
        functools.partial(_mla_attn_body, tq=tq),
        grid=(nb, MLA_HEADS // hp, nq),
        in_specs=[pl.BlockSpec((tq, hp * MLA_QK_PAD), lambda b, h, i: (b * nq + i, h)),
                  pl.BlockSpec((seq, hp * MLA_QK_PAD), lambda b, h, i: (b, h)),
                  pl.BlockSpec((nq, hp * MLA_V, tq), lambda b, h, i: (b, h, 0))],
        out_specs=pl.BlockSpec((tq, hp * MLA_V), lambda b, h, i: (b * nq + i, h)),
        out_shape=jax.ShapeDtypeStruct((nb * seq, MLA_HEADS * MLA_V), BF16),
        scratch_shapes=[pltpu.VMEM((hp, 1, tq), F32), pltpu.VMEM((hp, 1, tq), F32),
                        pltpu.VMEM((hp, MLA_V, tq), F32)],
        compiler_params=_cparams(("parallel", "parallel", "arbitrary"), V7X_VMEM_LIMIT),
        name="mla_attn",
    )(q, k, vt)


def _sb_attn_body(q_ref, k_ref, v_ref, tri_ref, o_ref, r_scr, acc_scr, *, tq, scale):
    i = pl.program_id(2)
    tri = tri_ref[...]
    r_scr[...] = jnp.zeros(r_scr.shape, F32)
    acc_scr[...] = jnp.zeros(acc_scr.shape, F32)

    def block(j, diagonal):
        start = pl.multiple_of(j * tq, tq)
        heads = range(ATTN_HEADS_PER_STEP)
        cols = [slice(hh * SB_DIM, (hh + 1) * SB_DIM) for hh in heads]
        z = [lax.dot_general(k_ref[pl.ds(start, tq), cols[hh]], q_ref[:, cols[hh]], _NT,
                             preferred_element_type=F32) * scale for hh in heads]
        log_not = [-(jnp.maximum(x, 0.0) + jnp.log1p(jnp.exp(-jnp.abs(x)))) for x in z]
        if diagonal:
            strict = (lax.broadcasted_iota(jnp.int32, z[0].shape, 0)
                      < lax.broadcasted_iota(jnp.int32, z[0].shape, 1))
            log_not = [jnp.where(strict, x, 0.0) for x in log_not]
        hi = [x.astype(BF16) for x in log_not]
        lo = [(x - h.astype(F32)).astype(BF16) for x, h in zip(log_not, hi)]
        later = [jnp.dot(tri, hi[hh], preferred_element_type=F32)
                 + jnp.dot(tri, lo[hh], preferred_element_type=F32) + r_scr[hh] for hh in heads]
        w = [jnp.exp(z[hh] + log_not[hh] + later[hh]) for hh in heads]
        if diagonal:
            w = [jnp.where(strict, x, 0.0) for x in w]
        for hh in heads:
            v = v_ref[pl.ds(start, tq), cols[hh]]
            acc_scr[hh] += jnp.dot(v.T, w[hh].astype(BF16), preferred_element_type=F32)
            r_scr[hh] += jnp.sum(log_not[hh], axis=0, keepdims=True)

    block(i, True)

    def cond(carry):
        j, rmax = carry
        return jnp.logical_and(j >= 0, rmax > EXP_UNDERFLOW)

    def body(carry):
        j, _ = carry
        block(j, False)
        return j - 1, jnp.max(r_scr[...])

    lax.while_loop(cond, body, (i - 1, jnp.max(r_scr[...])))
    for hh in range(ATTN_HEADS_PER_STEP):
        o_ref[:, hh * SB_DIM:(hh + 1) * SB_DIM] = acc_scr[hh].T.astype(o_ref.dtype)


def _sb_attn(proj, nb, seq):
    tq = min(256, seq)
    nq = seq // tq
    hp = ATTN_HEADS_PER_STEP
    cb = COL_SB // (hp * SB_DIM)
    hb = SB_HEADS // hp
    tri = (jnp.arange(tq)[None, :] > jnp.arange(tq)[:, None]).astype(BF16)
    return pl.pallas_call(
        functools.partial(_sb_attn_body, tq=tq, scale=float(SB_DIM ** -0.5)),
        grid=(nb, hb, nq),
        in_specs=[pl.BlockSpec((tq, hp * SB_DIM), lambda b, h, i: (b * nq + i, cb + h)),
                  pl.BlockSpec((seq, hp * SB_DIM), lambda b, h, i: (b, cb + hb + h)),
                  pl.BlockSpec((seq, hp * SB_DIM), lambda b, h, i: (b, cb + 2 * hb + h)),
                  pl.BlockSpec((tq, tq), lambda b, h, i: (0, 0))],
        out_specs=pl.BlockSpec((tq, hp * SB_DIM), lambda b, h, i: (b * nq + i, h)),
        out_shape=jax.ShapeDtypeStruct((nb * seq, SB_WIDTH), BF16),
        scratch_shapes=[pltpu.VMEM((hp, 1, tq), F32), pltpu.VMEM((hp, SB_DIM, tq), F32)],
        compiler_params=_cparams(("parallel", "parallel", "arbitrary"), V7X_VMEM_LIMIT),
        name="sb_attn",
    )(proj, proj, proj, tri)


def _mix_out_body(a_ref, b_ref, ga_ref, gb_ref, x_ref, wa_ref, wb_ref, wo_ref, gt_ref, g2_ref, sc_ref, sh_ref,
                  x1_ref, h2_ref):
    ya = jnp.dot(a_ref[...], wa_ref[...], preferred_element_type=F32)
    yb = jnp.dot(b_ref[...], wb_ref[...], preferred_element_type=F32)
    merged = (jax.nn.sigmoid(ga_ref[...].astype(F32)) * ya
              + jax.nn.sigmoid(gb_ref[...].astype(F32)) * yb)
    o = jnp.dot(merged.astype(BF16), wo_ref[...], preferred_element_type=F32)
    x1 = x_ref[...] + gt_ref[0] * o
    x1_ref[...] = x1
    h2 = (_rms(x1) * g2_ref[...]) * (1.0 + sc_ref[0]) + sh_ref[0]
    h2_ref[...] = h2.astype(h2_ref.dtype)


def _mix_out(att_a, att_b, proj, x2, wa, wb, wo, gt1, g2, sc2, sh2, seq):
    t, d = x2.shape
    tm = min(256, seq)
    per_b = seq // tm
    gcol = COL_GATE // d
    const = lambda a: pl.BlockSpec(a.shape, lambda i: (0,) * a.ndim, pipeline_mode=pl.Buffered(1))
    mod = lambda: pl.BlockSpec((1, 1, d), lambda i: (i // per_b, 0, 0))
    return pl.pallas_call(
        _mix_out_body,
        grid=(t // tm,),
        in_specs=[pl.BlockSpec((tm, att_a.shape[1]), lambda i: (i, 0)),
                  pl.BlockSpec((tm, att_b.shape[1]), lambda i: (i, 0)),
                  pl.BlockSpec((tm, d), lambda i: (i, gcol)),
                  pl.BlockSpec((tm, d), lambda i: (i, gcol + 1)),
                  pl.BlockSpec((tm, d), lambda i: (i, 0)),
                  const(wa), const(wb), const(wo), mod(),
                  pl.BlockSpec((1, d), lambda i: (0, 0)), mod(), mod()],
        out_specs=[pl.BlockSpec((tm, d), lambda i: (i, 0)),
                   pl.BlockSpec((tm, d), lambda i: (i, 0))],
        out_shape=[jax.ShapeDtypeStruct((t, d), F32), jax.ShapeDtypeStruct((t, d), BF16)],
        compiler_params=_cparams(("parallel",), V7X_VMEM_LIMIT),
        name="mix_out",
    )(att_a, att_b, proj, proj, x2, wa, wb, wo, gt1, g2.reshape(1, d), sc2, sh2)


def _split(a):
    hi = a.astype(BF16)
    return hi, (a - hi.astype(F32)).astype(BF16)


def _dot3_nt(a, b):
    ah, al = _split(a)
    bh, bl = _split(b)
    dg = lambda u, w: lax.dot_general(u, w, _NT, preferred_element_type=F32)
    return dg(ah, bh) + (dg(ah, bl) + dg(al, bh))


def _order(v, i, j):
    v[i], v[j] = jnp.maximum(v[i], v[j]), jnp.minimum(v[i], v[j])


def _bitonic_sort_desc(v):
    n = len(v)
    k = 2
    while k <= n:
        j = k // 2
        while j >= 1:
            for i in range(n):
                l = i ^ j
                if l > i:
                    if (i & k) == 0:
                        _order(v, i, l)
                    else:
                        _order(v, l, i)
            j //= 2
        k *= 2


def _bitonic_merge_desc(v):
    n = len(v)
    j = n // 2
    while j >= 1:
        for i in range(n):
            l = i ^ j
            if l > i:
                _order(v, i, l)
        j //= 2


def _merge_sublanes(v):
    n = len(v)
    for shift in (4, 2, 1):
        other = [pltpu.roll(x, shift, axis=0) for x in v]
        v = [jnp.maximum(v[k], other[n - 1 - k]) for k in range(n)]
        _bitonic_merge_desc(v)
    return v


def _top16_rows(s):
    v = [s[8 * k:8 * (k + 1), :] for k in range(s.shape[0] // 8)]
    _bitonic_sort_desc(v)
    return _merge_sublanes(v)


def _peer_score_body(h_ref, wq_ref, keys_ref, s1_ref, p1_ref, s2_ref, p2_ref, tau_ref, q0_scr, q1_scr):
    j = pl.program_id(0)

    @pl.when(j == 0)
    def _():
        q1_scr[...] = jnp.zeros(q1_scr.shape, q1_scr.dtype)

    @pl.when(j % 2 == 0)
    def _():
        _peer_rank(h_ref, wq_ref, q1_scr, q0_scr, keys_ref, s1_ref, p1_ref, s2_ref, p2_ref, tau_ref)

    @pl.when(j % 2 == 1)
    def _():
        _peer_rank(h_ref, wq_ref, q0_scr, q1_scr, keys_ref, s1_ref, p1_ref, s2_ref, p2_ref, tau_ref)


def _peer_rank(h_ref, wq_ref, q, q_next, keys_ref, s1_ref, p1_ref, s2_ref, p2_ref, tau_ref):
    half = PEER_DKEY // 2
    for h in range(PEER_HEADS):
        base = h * PEER_DKEY
        q_next[:, base:base + PEER_DKEY] = jnp.dot(h_ref[...], wq_ref[:, base:base + PEER_DKEY],
                                                   preferred_element_type=F32)
        s1 = _dot3_nt(keys_ref[h, 0], q[:, base:base + half])
        s2 = _dot3_nt(keys_ref[h, 1], q[:, base + half:base + PEER_DKEY])
        v1 = _top16_rows(s1)
        v2 = _top16_rows(s2)
        sub = lax.broadcasted_iota(jnp.int32, v2[0].shape, 0)
        v2_lo, v2_hi = v2[0], v2[8]
        for b in range(1, 8):
            v2_lo = jnp.where(sub == b, v2[b], v2_lo)
            v2_hi = jnp.where(sub == b, v2[8 + b], v2_hi)
        cand = [jnp.where(sub < PEER_TOPK // (a + 1), v1[a] + v2_lo, -jnp.inf) for a in range(PEER_TOPK)]
        extra = v1[0] + v2_hi
        for a in range(PEER_TOPK):
            cand[a], extra = jnp.maximum(cand[a], extra), jnp.minimum(cand[a], extra)
        top = _merge_sublanes(cand)
        z = functools.reduce(jnp.add, [jnp.exp(c - top[0]) for c in top])
        rows = slice(h * PEER_NKEYS, (h + 1) * PEER_NKEYS)
        s1_ref[rows, :] = s1
        s2_ref[rows, :] = s2
        p1_ref[rows, :] = jnp.exp(s1 - (v1[0] + jnp.log(z))[0:1, :])
        p2_ref[rows, :] = jnp.exp(s2 - v2[0][0:1, :])
        tau_ref[h * 8:(h + 1) * 8, :] = top[PEER_TOPK - 1]


def _peer_score(h2, wq, keys):
    t, d = h2.shape
    tm = min(256, t)
    n = t // tm
    rows = PEER_HEADS * PEER_NKEYS
    row_spec = pl.BlockSpec((rows, tm), lambda j: (0, jnp.maximum(j - 1, 0)))
    row_shape = jax.ShapeDtypeStruct((rows, t), F32)
    return pl.pallas_call(
        _peer_score_body,
        grid=(n + 1,),
        in_specs=[pl.BlockSpec((tm, d), lambda j: (jnp.minimum(j, n - 1), 0)),
                  pl.BlockSpec(wq.shape, lambda j: (0, 0)),
                  pl.BlockSpec(keys.shape, lambda j: (0, 0, 0, 0))],
        out_specs=[row_spec, row_spec, row_spec, row_spec,
                   pl.BlockSpec((PEER_HEADS * 8, tm), lambda j: (0, jnp.maximum(j - 1, 0)))],
        out_shape=[row_shape, row_shape, row_shape, row_shape,
                   jax.ShapeDtypeStruct((PEER_HEADS * 8, t), F32)],
        scratch_shapes=[pltpu.VMEM((tm, PEER_HEADS * PEER_DKEY), F32),
                        pltpu.VMEM((tm, PEER_HEADS * PEER_DKEY), F32)],
        compiler_params=_cparams(("arbitrary",), V7X_VMEM_LIMIT),
        name="peer_score",
    )(h2, wq, keys)


def _gelu_tanh(x):
    return 0.5 * x * (1.0 + jnp.tanh(0.7978845608028654 * (x + 0.044715 * (x * x * x))))


BUILD_ROWS = 32
MXU_PIECE = 512


def _peer_dense_body(h_ref, u_ref, vt_ref, s1_ref, p1_ref, s2_ref, p2_ref, tau_ref, x1_ref, gt_ref, g_ref, o_ref,
                     acc_scr, a0_scr, a1_scr, st0_scr, st1_scr, s1b_scr, p1b_scr, *, groups, n_j, last):
    s = pl.program_id(0)
    n_pairs = pl.num_programs(0) - 2
    j_build = jnp.clip(s - 1, 0, n_pairs - 1) % n_j
    j_values = jnp.clip(s - 2, 0, n_pairs - 1) % n_j

    @pl.when(s == 0)
    def _():
        a1_scr[...] = jnp.zeros(a1_scr.shape, a1_scr.dtype)
        st1_scr[...] = jnp.zeros(st1_scr.shape, st1_scr.dtype)

    @pl.when(jnp.logical_or(s < 2, j_values == 0))
    def _():
        acc_scr[...] = jnp.zeros(acc_scr.shape, acc_scr.dtype)

    def run(a_prev, a_next, st_prev, st_next):
        tm = acc_scr.shape[1]

        def values_piece(c):
            rows = slice(c * MXU_PIECE, (c + 1) * MXU_PIECE)
            acc_scr[rows, :] += jnp.dot(vt_ref[rows, :], a_prev[...], preferred_element_type=F32)

        def scores_piece(c):
            ks = slice(c * MXU_PIECE, (c + 1) * MXU_PIECE)
            part = lax.dot_general(u_ref[:, ks], h_ref[:, ks], _NT, preferred_element_type=F32)
            if c == 0:
                st_next[...] = part
            else:
                st_next[...] += part

        def spread_rows(g):
            e1 = j_build * groups + g
            for h in range(PEER_HEADS):
                row = pl.ds(h * PEER_NKEYS + e1, 1)
                dst = slice((g * PEER_HEADS + h) * 8, (g * PEER_HEADS + h + 1) * 8)
                s1b_scr[dst, :] = jnp.broadcast_to(s1_ref[row, :], (8, tm))
                p1b_scr[dst, :] = jnp.broadcast_to(p1_ref[row, :], (8, tm))

        def build_chunk(g, r0):
            subs = range(r0, r0 + BUILD_ROWS, 8)
            w = [None] * len(subs)
            for h in range(PEER_HEADS):
                src = slice((g * PEER_HEADS + h) * 8, (g * PEER_HEADS + h + 1) * 8)
                s1b = s1b_scr[src, :]
                p1b = p1b_scr[src, :]
                tau = tau_ref[h * 8:(h + 1) * 8, :]
                for n, r in enumerate(subs):
                    e2 = slice(h * PEER_NKEYS + r, h * PEER_NKEYS + r + 8)
                    term = jnp.where(s1b + s2_ref[e2, :] >= tau, p1b * p2_ref[e2, :], 0.0)
                    w[n] = term if w[n] is None else w[n] + term
            blk = slice(g * PEER_NKEYS + r0, g * PEER_NKEYS + r0 + BUILD_ROWS)
            a_next[blk, :] = (_gelu_tanh(st_prev[blk, :]) * jnp.concatenate(w, axis=0)).astype(a_next.dtype)

        pieces = []
        for c in range(acc_scr.shape[0] // MXU_PIECE):
            pieces += [functools.partial(values_piece, c), functools.partial(scores_piece, c)]
        chunks = []
        for g in range(groups):
            chunks.append(functools.partial(spread_rows, g))
            chunks += [functools.partial(build_chunk, g, r0) for r0 in range(0, PEER_NKEYS, BUILD_ROWS)]
        per_piece = -(-len(chunks) // len(pieces))
        for k, piece in enumerate(pieces):
            piece()
            for chunk in chunks[k * per_piece:(k + 1) * per_piece]:
                chunk()

    @pl.when(s % 2 == 0)
    def _():
        run(a1_scr, a0_scr, st1_scr, st0_scr)

    @pl.when(s % 2 == 1)
    def _():
        run(a0_scr, a1_scr, st0_scr, st1_scr)

    @pl.when(jnp.logical_and(s >= 2, j_values == n_j - 1))
    def _():
        x2 = x1_ref[...] + gt_ref[0] * acc_scr[...].T
        o_ref[...] = _rms(x2) * g_ref[...] if last else x2


def _peer_dense(h2, u, vt, s1, p1, s2, p2, tau, x1, gt2, g, seq, last):
    t, d = h2.shape
    ne = u.shape[0]
    tm = min(512, seq)
    te = 512
    n_j = ne // te
    n_pairs = (t // tm) * n_j
    per_b = seq // tm
    groups = te // PEER_NKEYS
    rows = PEER_HEADS * PEER_NKEYS
    pair = lambda s, lag: jnp.clip(s - lag, 0, n_pairs - 1)
    row_spec = pl.BlockSpec((rows, tm), lambda s: (0, pair(s, 1) // n_j))
    return pl.pallas_call(
        functools.partial(_peer_dense_body, groups=groups, n_j=n_j, last=last),
        grid=(n_pairs + 2,),
        in_specs=[pl.BlockSpec((tm, d), lambda s: (pair(s, 0) // n_j, 0)),
                  pl.BlockSpec((te, d), lambda s: (pair(s, 0) % n_j, 0)),
                  pl.BlockSpec((d, te), lambda s: (0, pair(s, 2) % n_j)),
                  row_spec, row_spec, row_spec, row_spec,
                  pl.BlockSpec((PEER_HEADS * 8, tm), lambda s: (0, pair(s, 1) // n_j)),
                  pl.BlockSpec((tm, d), lambda s: (pair(s, 2) // n_j, 0)),
                  pl.BlockSpec((1, 1, d), lambda s: (pair(s, 2) // n_j // per_b, 0, 0)),
                  pl.BlockSpec((1, d), lambda s: (0, 0))],
        out_specs=pl.BlockSpec((tm, d), lambda s: (pair(s, 2) // n_j, 0)),
        out_shape=jax.ShapeDtypeStruct((t, d), F32),
        scratch_shapes=[pltpu.VMEM((d, tm), F32),
                        pltpu.VMEM((te, tm), BF16), pltpu.VMEM((te, tm), BF16),
                        pltpu.VMEM((te, tm), F32), pltpu.VMEM((te, tm), F32),
                        pltpu.VMEM((groups * PEER_HEADS * 8, tm), F32),
                        pltpu.VMEM((groups * PEER_HEADS * 8, tm), F32)],
        compiler_params=_cparams(("arbitrary",), V7X_VMEM_LIMIT),
        name="peer_dense",
    )(h2, u, vt, s1, p1, s2, p2, tau, x1, gt2, g.reshape(1, d))


def _rot_half_cols(w):
    half = MLA_ROPE // 2
    return jnp.concatenate([-w[..., half:], w[..., :half]], axis=-1)


def _prep_w_in(w_in):
    d = w_in.shape[0]
    o = 0
    cq = w_in[:, o:o + MLA_Q_RANK]; o += MLA_Q_RANK
    ckv = w_in[:, o:o + MLA_KV_RANK]; o += MLA_KV_RANK
    kr = w_in[:, o:o + MLA_ROPE]; o += MLA_ROPE
    rest = w_in[:, o:]
    pad = jnp.zeros((d, 128 - MLA_ROPE), w_in.dtype)
    return jnp.concatenate([cq, ckv, kr, pad, _rot_half_cols(kr), pad, rest], axis=1).astype(BF16)


def _prep_w_uq(w_uq):
    r = w_uq.shape[0]
    w = w_uq.reshape(r, MLA_HEADS, MLA_NOPE + MLA_ROPE)
    nope, rope = w[..., :MLA_NOPE], w[..., MLA_NOPE:]
    pad = jnp.zeros((r, MLA_HEADS, MLA_QK_PAD - MLA_NOPE - MLA_ROPE), w_uq.dtype)
    wq = jnp.concatenate([nope, rope, pad], axis=-1).reshape(r, MLA_HEADS * MLA_QK_PAD)
    wqr = jnp.concatenate([_rot_half_cols(rope), pad], axis=-1).reshape(r, MLA_HEADS * 128)
    return wq.astype(BF16), wqr.astype(BF16)


def _prep_w_ukv(w_ukv):
    r = w_ukv.shape[0]
    w = w_ukv.reshape(r, MLA_HEADS, MLA_NOPE + MLA_V)
    wk = w[..., :MLA_NOPE].reshape(r, MLA_HEADS * MLA_NOPE)
    wv = w[..., MLA_NOPE:].reshape(r, MLA_HEADS * MLA_V)
    return wk.astype(BF16), wv.astype(BF16)


def kernel(x, c, positions, ada_w, ada_b, norm1_g, norm2_g, w_in, mla_q_norm_g, mla_kv_norm_g, mla_w_uq, mla_w_ukv, w_branch_mla, w_branch_sb, w_out, peer_w_q, peer_sub_keys, peer_u, peer_v, final_norm_g):
    nb, seq, d = x.shape
    t = nb * seq
    x2 = x.reshape(t, d)
    depth = ada_w.shape[0]
    for l in range(depth):
        mod = _ada(c, ada_w[l], ada_b[l])
        sh1, sc1, gt1, sh2, sc2, gt2 = [m.reshape(nb, 1, d) for m in jnp.split(mod, N_ADA, axis=-1)]

        proj = _in_proj(x2, norm1_g[l], sc1, sh1, _prep_w_in(w_in[l]), seq)

        inv_freq = ROPE_THETA ** (-jnp.arange(0, MLA_ROPE, 2, dtype=F32) / MLA_ROPE)
        invf = jnp.concatenate([inv_freq, inv_freq, jnp.zeros((128 - MLA_ROPE,), F32)]).reshape(1, 128)
        pos = positions.astype(F32).reshape(t, 1)
        wq, wqr = _prep_w_uq(mla_w_uq[l])
        wk, wv = _prep_w_ukv(mla_w_ukv[l])
        tq = min(512, seq)
        q_a, k_a, vt_a = _mla_prep(proj, pos, invf, mla_q_norm_g[l].reshape(1, -1), mla_kv_norm_g[l].reshape(1, -1),
                                   wq, wqr, wk, wv.T, tq)
        att_a = _mla_attn(q_a, k_a, vt_a, nb, seq, tq)
        att_b = _sb_attn(proj, nb, seq)

        x2, h2 = _mix_out(att_a, att_b, proj, x2, w_branch_mla[l].astype(BF16), w_branch_sb[l].astype(BF16),
                          w_out[l].astype(BF16), gt1, norm2_g[l], sc2, sh2, seq)

        s1, p1, s2, p2, tau = _peer_score(h2, peer_w_q[l].astype(BF16), peer_sub_keys[l])
        x2 = _peer_dense(h2, peer_u[l].astype(BF16), peer_v[l].astype(BF16).T, s1, p1, s2, p2, tau,
                         x2, gt2, final_norm_g, seq, last=(l == depth - 1))
    return x2.reshape(nb, seq, d)
```
